```python
import math
import jax, jax.numpy as jnp
from jax import lax
import numpy as np

D_MODEL = 1024
BATCH = 32
SEQ = 2048
DEPTH = 4

CHUNK = 64
Q_BLOCK = 128
EPS = 1e-6
ROPE_THETA = 10000.0

CONV_DIM = D_MODEL
CONV_W = 3
SSD_HEADS = 16
SSD_HEAD_DIM = 64
SSD_DIM = SSD_HEADS * SSD_HEAD_DIM
SSD_GROUPS = 2
SSD_STATE = 64
SSD_CONV_W = 4
SSD_CONV_DIM = SSD_DIM + 2 * SSD_GROUPS * SSD_STATE
IN_EVEN = 3 * CONV_DIM + SSD_DIM + SSD_CONV_DIM + SSD_HEADS
MIX_EVEN = CONV_DIM + SSD_DIM
DIFF_HEADS = 8
DIFF_HEAD_DIM = 64
D_FF = 2816
N_EXPERTS = 8
TOP_K = 2
MOE_BLOCK = 128
N_EVEN = (DEPTH + 1) // 2
N_ODD = DEPTH // 2

kernel_name = "hybrid_conv_ssd_diffattn_moe_trunk"


def rmsnorm(x, g):
    xf = x.astype(jnp.float32)
    xf = xf * lax.rsqrt(jnp.mean(xf * xf, axis=-1, keepdims=True) + EPS)
    return (xf * g.astype(jnp.float32)).astype(x.dtype)


def causal_dwconv(x, w):
    k, c = w.shape
    return lax.conv_general_dilated(
        x, w.astype(x.dtype)[:, None, :], window_strides=(1,), padding=[(k - 1, 0)],
        dimension_numbers=("NWC", "WIO", "NWC"), feature_group_count=c)


def swiglu(x, w_gate, w_up, w_down):
    return (jax.nn.silu(x @ w_gate) * (x @ w_up)) @ w_down


def ssd_scan(x, dt, a, bm, cm):
    b, s, g, r, p = x.shape
    n = bm.shape[-1]
    nc = s // CHUNK
    x = x.reshape(b, nc, CHUNK, g, r, p)
    dt = dt.reshape(b, nc, CHUNK, g, r)
    bm = bm.reshape(b, nc, CHUNK, g, n)
    cm = cm.reshape(b, nc, CHUNK, g, n)
    xdt = x * dt[..., None]
    a_cs = jnp.cumsum(jnp.moveaxis(dt * a, 2, -1), axis=-1)
    idx = jnp.arange(CHUNK)
    tri = idx[:, None] >= idx[None, :]
    seg = a_cs[..., :, None] - a_cs[..., None, :]
    decay = jnp.exp(jnp.where(tri, seg, -jnp.inf))
    cb = jnp.einsum("bclgn,bcsgn->bcgls", cm, bm)
    y_diag = jnp.einsum("bcgrls,bcsgrp->bclgrp", cb[:, :, :, None] * decay, xdt)
    decay_states = jnp.exp(a_cs[..., -1:] - a_cs)
    xw = xdt * jnp.moveaxis(decay_states, -1, 2)[..., None]
    states = jnp.einsum("bclgn,bclgrp->bcgrpn", bm, xw)
    chunk_decay = jnp.exp(a_cs[..., -1])

    def step(h, inp):
        st, dec = inp
        return h * dec[..., None, None] + st, h

    h0 = jnp.zeros((b, g, r, p, n), x.dtype)
    _, prev = lax.scan(step, h0, (jnp.moveaxis(states, 1, 0), jnp.moveaxis(chunk_decay, 1, 0)))
    prev = jnp.moveaxis(prev, 0, 1)
    y_off = jnp.einsum("bclgn,bcgrpn->bclgrp", cm, prev) * jnp.moveaxis(jnp.exp(a_cs), -1, 2)[..., None]
    return (y_diag + y_off).reshape(b, s, g, r, p)


def conv_ssd_mixer(h, w_in, conv_w, ssd_conv_w, ssd_conv_b, dt_bias, a_log, d_skip, ssd_norm_g, w_out):
    b, s, _ = h.shape
    proj = h @ w_in
    cuts = np.cumsum([CONV_DIM, CONV_DIM, CONV_DIM, SSD_DIM, SSD_CONV_DIM]).tolist()
    bg, cg, hv, z, xbc, dt_raw = jnp.split(proj, cuts, axis=-1)
    y_a = bg * causal_dwconv(cg * hv, conv_w)
    xbc = jax.nn.silu(causal_dwconv(xbc, ssd_conv_w) + ssd_conv_b.astype(xbc.dtype))
    xs, bm, cm = jnp.split(xbc, [SSD_DIM, SSD_DIM + SSD_GROUPS * SSD_STATE], axis=-1)
    r = SSD_HEADS // SSD_GROUPS
    f32 = jnp.float32
    dt = jax.nn.softplus(dt_raw.astype(f32) + dt_bias.astype(f32))
    a = -jnp.exp(a_log.astype(f32))
    xs5 = xs.astype(f32).reshape(b, s, SSD_GROUPS, r, SSD_HEAD_DIM)
    y = ssd_scan(xs5, dt.reshape(b, s, SSD_GROUPS, r), a.reshape(SSD_GROUPS, r),
                 bm.astype(f32).reshape(b, s, SSD_GROUPS, SSD_STATE),
                 cm.astype(f32).reshape(b, s, SSD_GROUPS, SSD_STATE))
    y = y + d_skip.astype(f32).reshape(SSD_GROUPS, r)[:, :, None] * xs5
    y = y.reshape(b, s, SSD_DIM) * jax.nn.silu(z.astype(f32))
    yg = y.reshape(b, s, SSD_GROUPS, SSD_DIM // SSD_GROUPS)
    yg = yg * lax.rsqrt(jnp.mean(yg * yg, axis=-1, keepdims=True) + EPS)
    y_b = (yg.reshape(b, s, SSD_DIM) * ssd_norm_g.astype(f32)).astype(h.dtype)
    return jnp.concatenate([y_a, y_b], axis=-1) @ w_out


def rope_tables(seq):
    inv = 1.0 / (ROPE_THETA ** (jnp.arange(0, DIFF_HEAD_DIM, 2, dtype=jnp.float32) / DIFF_HEAD_DIM))
    ang = jnp.arange(seq, dtype=jnp.float32)[:, None] * inv[None, :]
    return jnp.cos(ang), jnp.sin(ang)


def apply_rope(x, cos, sin):
    xf = x.astype(jnp.float32)
    half = xf.shape[-1] // 2
    x1, x2 = xf[..., :half], xf[..., half:]
    c, s_ = cos[None, :, None, :], sin[None, :, None, :]
    return jnp.concatenate([x1 * c - x2 * s_, x2 * c + x1 * s_], axis=-1).astype(x.dtype)


def diff_attention(h, w_qkv, q_norm, k_norm, lam_q1, lam_k1, lam_q2, lam_k2, sub_norm, w_o, lambda_init, cos, sin):
    b, s, _ = h.shape
    q, k, v = jnp.split(h @ w_qkv, 3, axis=-1)
    q = apply_rope(rmsnorm(q.reshape(b, s, 2 * DIFF_HEADS, DIFF_HEAD_DIM), q_norm), cos, sin)
    k = apply_rope(rmsnorm(k.reshape(b, s, 2 * DIFF_HEADS, DIFF_HEAD_DIM), k_norm), cos, sin)
    v = v.reshape(b, s, DIFF_HEADS, 2 * DIFF_HEAD_DIM)
    q = jnp.transpose(q, (0, 2, 1, 3))
    k = jnp.transpose(k, (0, 2, 1, 3))
    v = jnp.transpose(v, (0, 2, 1, 3))
    f32 = jnp.float32
    lam = (jnp.exp(jnp.sum(lam_q1.astype(f32) * lam_k1.astype(f32)))
           - jnp.exp(jnp.sum(lam_q2.astype(f32) * lam_k2.astype(f32))) + lambda_init)
    scale = 1.0 / math.sqrt(DIFF_HEAD_DIM)
    outs = []
    for q0 in range(0, s, Q_BLOCK):
        q1 = q0 + Q_BLOCK
        qb, kb, vb = q[:, :, q0:q1], k[:, :, :q1], v[:, :, :q1]
        sc = jnp.einsum("bhqd,bhkd->bhqk", qb, kb).astype(f32) * scale
        qpos = jnp.arange(q0, q1)
        kpos = jnp.arange(q1)
        mask = (kpos[None, :] // CHUNK) <= (qpos[:, None] // CHUNK)
        p = jax.nn.softmax(jnp.where(mask, sc, -jnp.inf), axis=-1)
        p = p.reshape(b, DIFF_HEADS, 2, Q_BLOCK, q1)
        a = p[:, :, 0] - lam * p[:, :, 1]
        outs.append(jnp.einsum("bhqk,bhkd->bhqd", a.astype(v.dtype), vb))
    o = jnp.concatenate(outs, axis=2)
    o = rmsnorm(o, sub_norm) * (1.0 - lambda_init)
    o = jnp.transpose(o, (0, 2, 1, 3)).reshape(b, s, DIFF_HEADS * 2 * DIFF_HEAD_DIM)
    return o @ w_o


def moe_swiglu(h, w_router, w_gate_e, w_up_e, w_down_e):
    b, s, d = h.shape
    t = b * s
    xf = h.reshape(t, d)
    logits = (xf @ w_router).astype(jnp.float32)
    top_logit, top_idx = lax.top_k(logits, TOP_K)
    gates = jax.nn.softmax(top_logit, axis=-1)
    flat_e = top_idx.reshape(-1)
    flat_tok = jnp.repeat(jnp.arange(t, dtype=jnp.int32), TOP_K)
    flat_g = gates.reshape(-1)
    order = jnp.argsort(flat_e)
    se, st, sg = flat_e[order], flat_tok[order], flat_g[order]
    counts = jnp.bincount(flat_e, length=N_EXPERTS)
    starts = jnp.cumsum(counts) - counts
    padded = (counts + MOE_BLOCK - 1) // MOE_BLOCK * MOE_BLOCK
    pends = jnp.cumsum(padded)
    pstarts = pends - padded
    n_assign = t * TOP_K
    dest = pstarts[se] + (jnp.arange(n_assign) - starts[se])
    cap = n_assign + N_EXPERTS * MOE_BLOCK
    n_blocks = cap // MOE_BLOCK
    slot_tok = jnp.full((cap,), t, jnp.int32).at[dest].set(st)
    slot_gate = jnp.zeros((cap,), jnp.float32).at[dest].set(sg)
    block_e = jnp.minimum(jnp.searchsorted(pends, jnp.arange(n_blocks) * MOE_BLOCK, side="right"), N_EXPERTS - 1)
    xpad = jnp.concatenate([xf, jnp.zeros((1, d), xf.dtype)], axis=0)
    xb = xpad[slot_tok].reshape(n_blocks, MOE_BLOCK, d)

    def expert_block(args):
        xblk, e = args
        return swiglu(xblk, w_gate_e[e], w_up_e[e], w_down_e[e])

    yb = lax.map(expert_block, (xb, block_e)).reshape(cap, d)
    out = jnp.zeros((t + 1, d), jnp.float32).at[slot_tok].add(yb.astype(jnp.float32) * slot_gate[:, None])
    return out[:t].astype(h.dtype).reshape(b, s, d)


def setup_inputs(seed: int = 0) -> dict:
    key = jax.random.key(seed)
    ks = iter(jax.random.split(key, 48))
    f32 = jnp.float32

    def normal(shape, scale):
        return jax.random.normal(next(ks), shape, f32) * scale

    def gain(shape):
        return 1.0 + normal(shape, 0.02)

    D, F, E = D_MODEL, D_FF, N_EXPERTS
    dt0 = jnp.exp(jax.random.uniform(next(ks), (N_EVEN, SSD_HEADS), f32) * (math.log(0.1) - math.log(0.001)) + math.log(0.001))
    inp = {
        "x": normal((BATCH, SEQ, D), 1.0),
        "even_norm_mix": gain((N_EVEN, D)),
        "even_w_in": normal((N_EVEN, D, IN_EVEN), D ** -0.5),
        "even_conv_w": normal((N_EVEN, CONV_W, CONV_DIM), CONV_W ** -0.5),
        "ssd_conv_w": normal((N_EVEN, SSD_CONV_W, SSD_CONV_DIM), SSD_CONV_W ** -0.5),
        "ssd_conv_b": normal((N_EVEN, SSD_CONV_DIM), 0.02),
        "ssd_dt_bias": dt0 + jnp.log(-jnp.expm1(-dt0)),
        "ssd_a_log": jnp.log(jax.random.uniform(next(ks), (N_EVEN, SSD_HEADS), f32, 1.0, 16.0)),
        "ssd_d": gain((N_EVEN, SSD_HEADS)),
        "ssd_norm_g": gain((N_EVEN, SSD_DIM)),
        "even_w_out": normal((N_EVEN, MIX_EVEN, D), MIX_EVEN ** -0.5),
        "even_norm_ffn": gain((N_EVEN, D)),
        "ffn_w_gate": normal((N_EVEN, D, F), D ** -0.5),
        "ffn_w_up": normal((N_EVEN, D, F), D ** -0.5),
        "ffn_w_down": normal((N_EVEN, F, D), F ** -0.5),
        "odd_norm_mix": gain((N_ODD, D)),
        "attn_w_qkv": normal((N_ODD, D, 3 * D), D ** -0.5),
        "attn_q_norm": gain((N_ODD, DIFF_HEAD_DIM)),
        "attn_k_norm": gain((N_ODD, DIFF_HEAD_DIM)),
        "attn_lambda_q1": normal((N_ODD, DIFF_HEAD_DIM), 0.1),
        "attn_lambda_k1": normal((N_ODD, DIFF_HEAD_DIM), 0.1),
        "attn_lambda_q2": normal((N_ODD, DIFF_HEAD_DIM), 0.1),
        "attn_lambda_k2": normal((N_ODD, DIFF_HEAD_DIM), 0.1),
        "attn_sub_norm": gain((N_ODD, 2 * DIFF_HEAD_DIM)),
        "attn_w_o": normal((N_ODD, D, D), D ** -0.5),
        "odd_norm_ffn": gain((N_ODD, D)),
        "moe_w_router": normal((N_ODD, D, E), D ** -0.5),
        "moe_w_gate": normal((N_ODD, E, D, F), D ** -0.5),
        "moe_w_up": normal((N_ODD, E, D, F), D ** -0.5),
        "moe_w_down": normal((N_ODD, E, F, D), F ** -0.5),
    }
    return inp


def reference(x, even_norm_mix, even_w_in, even_conv_w, ssd_conv_w, ssd_conv_b, ssd_dt_bias, ssd_a_log,
              ssd_d, ssd_norm_g, even_w_out, even_norm_ffn, ffn_w_gate, ffn_w_up, ffn_w_down,
              odd_norm_mix, attn_w_qkv, attn_q_norm, attn_k_norm, attn_lambda_q1, attn_lambda_k1,
              attn_lambda_q2, attn_lambda_k2, attn_sub_norm, attn_w_o, odd_norm_ffn,
              moe_w_router, moe_w_gate, moe_w_up, moe_w_down):
    cos, sin = rope_tables(x.shape[1])
    for layer in range(DEPTH):
        i = layer // 2
        if layer % 2 == 0:
            h = rmsnorm(x, even_norm_mix[i])
            x = x + conv_ssd_mixer(h, even_w_in[i], even_conv_w[i], ssd_conv_w[i], ssd_conv_b[i],
                                   ssd_dt_bias[i], ssd_a_log[i], ssd_d[i], ssd_norm_g[i], even_w_out[i])
            h = rmsnorm(x, even_norm_ffn[i])
            x = x + swiglu(h, ffn_w_gate[i], ffn_w_up[i], ffn_w_down[i])
        else:
            lambda_init = 0.8 - 0.6 * math.exp(-0.3 * layer)
            h = rmsnorm(x, odd_norm_mix[i])
            x = x + diff_attention(h, attn_w_qkv[i], attn_q_norm[i], attn_k_norm[i], attn_lambda_q1[i],
                                   attn_lambda_k1[i], attn_lambda_q2[i], attn_lambda_k2[i], attn_sub_norm[i],
                                   attn_w_o[i], lambda_init, cos, sin)
            h = rmsnorm(x, odd_norm_ffn[i])
            x = x + moe_swiglu(h, moe_w_router[i], moe_w_gate[i], moe_w_up[i], moe_w_down[i])
    return x
```

```python
import functools
import math

import jax
import jax.numpy as jnp
from jax import lax
from jax.experimental import pallas as pl
from jax.experimental.pallas import tpu as pltpu

F32 = jnp.float32
BF16 = jnp.bfloat16
I32 = jnp.int32

EPS = 1e-6
ROPE_THETA = 10000.0
CHUNK = 64
HEAD = 64
LANES = 128
SSD_HEADS = 16
SSD_GROUPS = 2
N_EXPERTS = 8
VMEM_LIMIT = 48 * 1024 * 1024
NEG = -1e30


def _params(*sem):
    return pltpu.CompilerParams(dimension_semantics=sem, vmem_limit_bytes=VMEM_LIMIT)


def _rms(x, g):
    ms = jnp.mean(x * x, axis=-1, keepdims=True)
    return x * lax.rsqrt(ms + EPS) * g


def _silu(x):
    return x * (1.0 / (1.0 + jnp.exp(-x)))


def _dot(a, b):
    return jnp.dot(a, b, preferred_element_type=F32)


def _dot_nt(a, b):
    return lax.dot_general(a, b, (((1,), (1,)), ((), ())), preferred_element_type=F32)


def _dot_tn(a, b):
    return lax.dot_general(a, b, (((0,), (0,)), ((), ())), preferred_element_type=F32)


def _norm_proj_kernel(x_ref, g_ref, w_ref, wdt_ref, o_ref, odt_ref, h_ref):
    @pl.when(pl.program_id(1) == 0)
    def _():
        h = _rms(x_ref[...], g_ref[...]).astype(BF16)
        h_ref[...] = h
        odt_ref[...] = _dot(h, wdt_ref[...])

    o_ref[...] = _dot(h_ref[...], w_ref[...]).astype(o_ref.dtype)


def norm_proj(x, g, w_main, w_dt, *, tm, tn):
    t, d = x.shape
    n = w_main.shape[1]
    return pl.pallas_call(
        _norm_proj_kernel,
        grid=(t // tm, n // tn),
        in_specs=[
            pl.BlockSpec((tm, d), lambda i, j: (i, 0)),
            pl.BlockSpec((1, d), lambda i, j: (0, 0)),
            pl.BlockSpec((d, tn), lambda i, j: (0, j)),
            pl.BlockSpec((d, LANES), lambda i, j: (0, 0)),
        ],
        out_specs=[
            pl.BlockSpec((tm, tn), lambda i, j: (i, j)),
            pl.BlockSpec((tm, LANES), lambda i, j: (i, 0)),
        ],
        out_shape=[
            jax.ShapeDtypeStruct((t, n), BF16),
            jax.ShapeDtypeStruct((t, LANES), F32),
        ],
        scratch_shapes=[pltpu.VMEM((tm, d), BF16)],
        compiler_params=_params("parallel", "arbitrary"),
        name="norm_proj",
    )(x, g, w_main, w_dt)


HALO = 8


def _split3(v):
    hi = v.astype(BF16)
    r = v - hi.astype(F32)
    mid = r.astype(BF16)
    lo = (r - mid.astype(F32)).astype(BF16)
    return hi, mid, lo


def _expand(v, e, terms):
    parts = _split3(v)[:terms]
    out = _dot(parts[0], e)
    for p in parts[1:]:
        out = out + _dot(p, e)
    return out


def _mixer_kernel(bg_ref, cg_ref, hv_ref, z_ref, xs_ref, bc_ref, dtr_ref,
                  cw_ref, xw_ref, bcw_ref, xb_ref, bcb_ref, dtb_ref, alog_ref,
                  dskip_ref, ng_ref, e_ref,
                  o_ref, ubuf, xbuf, bcbuf, state, ybuf):
    L = o_ref.shape[0]
    d = bg_ref.shape[1]
    nchunk = L // CHUNK

    @pl.when(pl.program_id(1) == 0)
    def _():
        ubuf[0:HALO, :] = jnp.zeros((HALO, ubuf.shape[1]), F32)
        xbuf[0:HALO, :] = jnp.zeros((HALO, xbuf.shape[1]), F32)
        bcbuf[0:HALO, :] = jnp.zeros((HALO, bcbuf.shape[1]), F32)
        state[...] = jnp.zeros(state.shape, F32)

    u = cg_ref[...].astype(F32) * hv_ref[...].astype(F32)
    ubuf[HALO:HALO + L, :] = u
    cw = cw_ref[...]
    conv = (cw[0:1] * ubuf[HALO - 2:HALO - 2 + L, :]
            + cw[1:2] * ubuf[HALO - 1:HALO - 1 + L, :]
            + cw[2:3] * u)
    o_ref[:, 0:d] = (bg_ref[...].astype(F32) * conv).astype(o_ref.dtype)
    ubuf[0:HALO, :] = ubuf[L:L + HALO, :]

    def conv4(src_ref, buf, w_ref, b_ref):
        buf[HALO:HALO + L, :] = src_ref[...].astype(F32)
        w = w_ref[...]
        acc = b_ref[...] + w[3:4] * buf[HALO:HALO + L, :]
        for k in range(3):
            acc = acc + w[k:k + 1] * buf[HALO - 3 + k:HALO - 3 + k + L, :]
        buf[0:HALO, :] = buf[L:L + HALO, :]
        return _silu(acc)

    xc = conv4(xs_ref, xbuf, xw_ref, xb_ref)
    bcc = conv4(bc_ref, bcbuf, bcw_ref, bcb_ref)

    e = e_ref[...]
    dtv = dtr_ref[...] + dtb_ref[...]
    dt = jnp.maximum(dtv, 0.0) + jnp.log(1.0 + jnp.exp(-jnp.abs(dtv)))
    a = -jnp.exp(alog_ref[...])
    da = dt * a

    ri = lax.broadcasted_iota(I32, (L, L), 0)
    ci = lax.broadcasted_iota(I32, (L, L), 1)
    tri = jnp.where((ri >= ci) & ((ri // CHUNK) == (ci // CHUNK)), 1.0, 0.0).astype(BF16)
    d_hi, d_mid, d_lo = _split3(da)
    acs = _dot(tri, d_hi) + _dot(tri, d_mid) + _dot(tri, d_lo)

    last = jnp.concatenate(
        [jnp.broadcast_to(acs[c * CHUNK + CHUNK - 1:c * CHUNK + CHUNK, :], (CHUNK, LANES))
         for c in range(nchunk)], axis=0)
    col_all = _expand(acs, e, 3)
    e_all = jnp.exp(col_all)
    ds_all = jnp.exp(_expand(last, e, 3) - col_all)
    xdt = xc * _expand(dt, e, 3)
    xwt = (xdt * ds_all).astype(BF16)
    xdt_b = xdt.astype(BF16)

    lane = lax.broadcasted_iota(I32, (CHUNK, LANES), 1)
    row = lax.broadcasted_iota(I32, (CHUNK, LANES), 0)
    low_half = lane < HEAD
    tri2 = row >= (lane % HEAD)
    lane2 = lax.broadcasted_iota(I32, (2 * CHUNK, LANES), 1)
    row2 = lax.broadcasted_iota(I32, (2 * CHUNK, LANES), 0)
    bd_mask = (lane2 < HEAD) == (row2 < CHUNK)
    srow = lax.broadcasted_iota(I32, state.shape, 0)
    scol = lax.broadcasted_iota(I32, state.shape, 1)
    state_mask = (srow // HEAD) == (scol // (d // SSD_GROUPS))

    for c in range(nchunk):
        r0 = c * CHUNK
        bm = bcc[r0:r0 + CHUNK, 0:LANES]
        cm = bcc[r0:r0 + CHUNK, LANES:2 * LANES]
        bm_b = bm.astype(BF16)
        cm_b = cm.astype(BF16)
        acs_c = acs[r0:r0 + CHUNK, :]
        acs_t = jnp.concatenate([acs_c, acs_c], axis=0).T
        st = state[...]
        y_off = _dot(cm_b, st.astype(BF16)) * e_all[r0:r0 + CHUNK, :]
        bm2 = jnp.concatenate([bm_b, bm_b], axis=0)
        cb2 = []
        for g in range(SSD_GROUPS):
            cmm = jnp.where((lane // HEAD) == g, cm, 0.0).astype(BF16)
            cb2.append(_dot_nt(cmm, bm2))
        yd = []
        for q in range(SSD_HEADS // 2):
            colq = col_all[r0:r0 + CHUNK, q * LANES:(q + 1) * LANES]
            rowq = jnp.where(low_half[0:1], acs_t[2 * q:2 * q + 1, :], acs_t[2 * q + 1:2 * q + 2, :])
            dec = jnp.exp(jnp.where(tri2, colq - rowq, NEG))
            m2 = (cb2[q // (SSD_HEADS // 2 // SSD_GROUPS)] * dec).astype(BF16)
            xq = xdt_b[r0:r0 + CHUNK, q * LANES:(q + 1) * LANES]
            xq2 = jnp.concatenate([xq, xq], axis=0)
            xbd = jnp.where(bd_mask, xq2, jnp.zeros_like(xq2))
            yd.append(_dot(m2, xbd))
        y = jnp.concatenate(yd, axis=1) + y_off + dskip_ref[...] * xc[r0:r0 + CHUNK, :]
        ybuf[r0:r0 + CHUNK, :] = y
        s_new = _dot_tn(bm_b, xwt[r0:r0 + CHUNK, :])
        state[...] = st * e_all[r0 + CHUNK - 1:r0 + CHUNK, :] + jnp.where(state_mask, s_new, 0.0)

    y = ybuf[...] * _silu(z_ref[...].astype(F32))
    half = d // SSD_GROUPS
    ng = ng_ref[...]
    for g in range(SSD_GROUPS):
        yg = y[:, g * half:(g + 1) * half]
        o_ref[:, d + g * half:d + (g + 1) * half] = _rms(yg, ng[:, g * half:(g + 1) * half]).astype(o_ref.dtype)


def conv_ssd_mixer(proj, dt_raw, cw, xw, bcw, xb, bcb, dtb, alog, dskip, ng, expand, *, batch, seq, blk):
    t = proj.shape[0]
    d = 1024
    nb = seq // blk
    row = lambda b, s: b * nb + s
    full = lambda shape: pl.BlockSpec(shape, lambda b, s: (0, 0))
    return pl.pallas_call(
        _mixer_kernel,
        grid=(batch, nb),
        in_specs=[
            pl.BlockSpec((blk, d), lambda b, s: (row(b, s), 0)),
            pl.BlockSpec((blk, d), lambda b, s: (row(b, s), 1)),
            pl.BlockSpec((blk, d), lambda b, s: (row(b, s), 2)),
            pl.BlockSpec((blk, d), lambda b, s: (row(b, s), 3)),
            pl.BlockSpec((blk, d), lambda b, s: (row(b, s), 4)),
            pl.BlockSpec((blk, 2 * LANES), lambda b, s: (row(b, s), 5 * d // (2 * LANES))),
            pl.BlockSpec((blk, LANES), lambda b, s: (row(b, s), 0)),
            full(cw.shape), full(xw.shape), full(bcw.shape), full(xb.shape), full(bcb.shape),
            full(dtb.shape), full(alog.shape), full(dskip.shape), full(ng.shape), full(expand.shape),
        ],
        out_specs=pl.BlockSpec((blk, 2 * d), lambda b, s: (row(b, s), 0)),
        out_shape=jax.ShapeDtypeStruct((t, 2 * d), BF16),
        scratch_shapes=[
            pltpu.VMEM((blk + HALO, d), F32),
            pltpu.VMEM((blk + HALO, d), F32),
            pltpu.VMEM((blk + HALO, 2 * LANES), F32),
            pltpu.VMEM((SSD_GROUPS * HEAD, d), F32),
            pltpu.VMEM((blk, d), F32),
        ],
        compiler_params=_params("parallel", "arbitrary"),
        name="conv_ssd_mixer",
    )(proj, proj, proj, proj, proj, proj, dt_raw, cw, xw, bcw, xb, bcb, dtb, alog, dskip, ng, expand)


def _proj_res_norm_kernel(a_ref, w_ref, x_ref, g_ref, xo_ref, ho_ref):
    y = x_ref[...] + _dot(a_ref[...], w_ref[...])
    xo_ref[...] = y
    ho_ref[...] = _rms(y, g_ref[...]).astype(ho_ref.dtype)


def proj_res_norm(a, w, x, g, *, tm, h_dtype):
    t, k = a.shape
    d = w.shape[1]
    return pl.pallas_call(
        _proj_res_norm_kernel,
        grid=(t // tm,),
        in_specs=[
            pl.BlockSpec((tm, k), lambda i: (i, 0)),
            pl.BlockSpec((k, d), lambda i: (0, 0)),
            pl.BlockSpec((tm, d), lambda i: (i, 0)),
            pl.BlockSpec((1, d), lambda i: (0, 0)),
        ],
        out_specs=[
            pl.BlockSpec((tm, d), lambda i: (i, 0)),
            pl.BlockSpec((tm, d), lambda i: (i, 0)),
        ],
        out_shape=[
            jax.ShapeDtypeStruct((t, d), F32),
            jax.ShapeDtypeStruct((t, d), h_dtype),
        ],
        compiler_params=_params("parallel"),
        name="proj_res_norm",
    )(a, w, x, g)


def _ffn_kernel(h_ref, x_ref, wg_ref, wu_ref, wd_ref, o_ref):
    @pl.when(pl.program_id(1) == 0)
    def _():
        o_ref[...] = x_ref[...]

    h = h_ref[...]
    g = _dot(h, wg_ref[...])
    u = _dot(h, wu_ref[...])
    o_ref[...] += _dot((_silu(g) * u).astype(BF16), wd_ref[...])


def ffn(h, x, wg, wu, wd, *, tm, tf):
    t, d = x.shape
    f = wg.shape[1]
    return pl.pallas_call(
        _ffn_kernel,
        grid=(t // tm, f // tf),
        in_specs=[
            pl.BlockSpec((tm, d), lambda i, j: (i, 0)),
            pl.BlockSpec((tm, d), lambda i, j: (i, 0)),
            pl.BlockSpec((d, tf), lambda i, j: (0, j)),
            pl.BlockSpec((d, tf), lambda i, j: (0, j)),
            pl.BlockSpec((tf, d), lambda i, j: (j, 0)),
        ],
        out_specs=pl.BlockSpec((tm, d), lambda i, j: (i, 0)),
        out_shape=jax.ShapeDtypeStruct((t, d), F32),
        compiler_params=_params("parallel", "arbitrary"),
        name="ffn",
    )(h, x, wg, wu, wd)


def _qkv_kernel(x_ref, g_ref, w_ref, hg_ref, cos_ref, sin_ref, p_ref, o_ref, h_ref):
    j = pl.program_id(1)

    @pl.when(j == 0)
    def _():
        h_ref[...] = _rms(x_ref[...], g_ref[...]).astype(BF16)

    y = _dot(h_ref[...], w_ref[...])

    @pl.when(j < 2)
    def _():
        d = y.shape[1]
        ms = _dot((y * y).astype(BF16), p_ref[...])
        yn = y * lax.rsqrt(ms + EPS) * hg_ref[0]
        lane = lax.broadcasted_iota(I32, y.shape, 1)
        partner = jnp.where((lane % HEAD) < HEAD // 2,
                            pltpu.roll(yn, d - HEAD // 2, 1), pltpu.roll(yn, HEAD // 2, 1))
        reps = d // LANES
        cos = jnp.concatenate([cos_ref[...]] * reps, axis=1)
        sin = jnp.concatenate([sin_ref[...]] * reps, axis=1)
        o_ref[...] = (yn * cos + partner * sin).astype(o_ref.dtype)

    @pl.when(j == 2)
    def _():
        o_ref[...] = y.astype(o_ref.dtype)


def qkv_proj(x, g, w, head_gain, cos_t, sin_t, pavg, *, tm, seq):
    t, d = x.shape
    ns = seq // tm
    return pl.pallas_call(
        _qkv_kernel,
        grid=(t // tm, 3),
        in_specs=[
            pl.BlockSpec((tm, d), lambda i, j: (i, 0)),
            pl.BlockSpec((1, d), lambda i, j: (0, 0)),
            pl.BlockSpec((d, d), lambda i, j: (0, j)),
            pl.BlockSpec((1, 1, d), lambda i, j: (jnp.minimum(j, 1), 0, 0)),
            pl.BlockSpec((tm, LANES), lambda i, j: (i % ns, 0)),
            pl.BlockSpec((tm, LANES), lambda i, j: (i % ns, 0)),
            pl.BlockSpec((d, d), lambda i, j: (0, 0)),
        ],
        out_specs=pl.BlockSpec((tm, d), lambda i, j: (i, j)),
        out_shape=jax.ShapeDtypeStruct((t, 3 * d), BF16),
        scratch_shapes=[pltpu.VMEM((tm, d), BF16)],
        compiler_params=_params("parallel", "arbitrary"),
        name="qkv_proj",
    )(x, g, w, head_gain, cos_t, sin_t, pavg)


def _attn_kernel(q_ref, k_ref, v_ref, lam_ref, sn_ref, o_ref, m_ref, l_ref, acc_ref, *, lambda_init):
    tq = q_ref.shape[0]
    i = pl.program_id(2)
    q = q_ref[...]
    lane = lax.broadcasted_iota(I32, q.shape, 1)
    zero = jnp.zeros_like(q)
    q2 = jnp.concatenate([jnp.where(lane < HEAD, q, zero), jnp.where(lane >= HEAD, q, zero)], axis=0)

    m_ref[...] = jnp.full(m_ref.shape, NEG, F32)
    l_ref[...] = jnp.zeros(l_ref.shape, F32)
    acc_ref[...] = jnp.zeros(acc_ref.shape, F32)

    def update(kb, masked):
        start = pl.multiple_of(kb * tq, tq)
        k = k_ref[pl.ds(start, tq), :]
        v = v_ref[pl.ds(start, tq), :]
        s = _dot_nt(q2, k)
        if masked:
            qi = lax.broadcasted_iota(I32, s.shape, 0) % tq
            ki = lax.broadcasted_iota(I32, s.shape, 1)
            s = jnp.where((ki // CHUNK) <= (qi // CHUNK), s, NEG)
        m_old = m_ref[...]
        m_new = jnp.maximum(m_old, jnp.max(s, axis=-1, keepdims=True))
        alpha = jnp.exp(m_old - m_new)
        p = jnp.exp(s - m_new)
        l_ref[...] = alpha * l_ref[...] + jnp.sum(p, axis=-1, keepdims=True)
        acc_ref[...] = alpha * acc_ref[...] + _dot(p.astype(BF16), v)
        m_ref[...] = m_new

    def body(kb, c):
        update(kb, False)
        return c

    lax.fori_loop(0, i, body, 0)
    update(i, True)

    lam_p = lam_ref[...]
    lam = (jnp.exp(jnp.sum(lam_p[0:1] * lam_p[1:2], axis=-1, keepdims=True))
           - jnp.exp(jnp.sum(lam_p[2:3] * lam_p[3:4], axis=-1, keepdims=True)) + lambda_init)
    o = acc_ref[...] / l_ref[...]
    o = o[0:tq] - lam * o[tq:2 * tq]
    o_ref[...] = (_rms(o, sn_ref[...]) * (1.0 - lambda_init)).astype(o_ref.dtype)


def diff_attention(qkv, lam_p, sub_norm, *, batch, seq, heads, tq, lambda_init):
    t = qkv.shape[0]
    nq = seq // tq
    return pl.pallas_call(
        functools.partial(_attn_kernel, lambda_init=lambda_init),
        grid=(batch, heads, nq),
        in_specs=[
            pl.BlockSpec((tq, LANES), lambda b, h, i: (b * nq + i, h)),
            pl.BlockSpec((seq, LANES), lambda b, h, i: (b, heads + h)),
            pl.BlockSpec((seq, LANES), lambda b, h, i: (b, 2 * heads + h)),
            pl.BlockSpec(lam_p.shape, lambda b, h, i: (0, 0)),
            pl.BlockSpec(sub_norm.shape, lambda b, h, i: (0, 0)),
        ],
        out_specs=pl.BlockSpec((tq, LANES), lambda b, h, i: (b * nq + i, h)),
        out_shape=jax.ShapeDtypeStruct((t, heads * LANES), BF16),
        scratch_shapes=[
            pltpu.VMEM((2 * tq, 1), F32),
            pltpu.VMEM((2 * tq, 1), F32),
            pltpu.VMEM((2 * tq, LANES), F32),
        ],
        compiler_params=_params("parallel", "parallel", "arbitrary"),
        name="diff_attention",
    )(qkv, qkv, qkv, lam_p, sub_norm)


def _router_kernel(h_ref, w_ref, meta_ref, cnt_ref, run_ref):
    tm = h_ref.shape[0]

    @pl.when(pl.program_id(0) == 0)
    def _():
        run_ref[...] = jnp.zeros(run_ref.shape, F32)

    h = h_ref[...]
    w = w_ref[...]
    h_hi = h.astype(BF16)
    h_lo = (h - h_hi.astype(F32)).astype(BF16)
    w_hi = w.astype(BF16)
    w_lo = (w - w_hi.astype(F32)).astype(BF16)
    logits = _dot(h_hi, w_hi) + (_dot(h_hi, w_lo) + _dot(h_lo, w_hi))
    lane = lax.broadcasted_iota(I32, logits.shape, 1)
    lane_f = lane.astype(F32)
    logits = jnp.where(lane < N_EXPERTS, logits, NEG)
    m1 = jnp.max(logits, axis=-1, keepdims=True)
    i1 = jnp.min(jnp.where(logits == m1, lane_f, float(LANES)), axis=-1, keepdims=True)
    rest = jnp.where(lane_f == i1, NEG, logits)
    m2 = jnp.max(rest, axis=-1, keepdims=True)
    i2 = jnp.min(jnp.where(rest == m2, lane_f, float(LANES)), axis=-1, keepdims=True)
    e2 = jnp.exp(m2 - m1)
    g1 = 1.0 / (1.0 + e2)
    g2 = e2 * g1

    oh1 = lane_f == i1
    oh2 = lane_f == i2
    both = jnp.where(oh1, 1.0, 0.0) + jnp.where(oh2, 1.0, 0.0)
    ri = lax.broadcasted_iota(I32, (tm, tm), 0)
    ci = lax.broadcasted_iota(I32, (tm, tm), 1)
    strict = jnp.where(ri > ci, 1.0, 0.0).astype(BF16)
    before = run_ref[...] + _dot(strict, both.astype(BF16))
    r1 = jnp.sum(jnp.where(oh1, before, 0.0), axis=-1, keepdims=True)
    r2 = jnp.sum(jnp.where(oh2, before, 0.0), axis=-1, keepdims=True)
    run_ref[...] += jnp.sum(both, axis=0, keepdims=True)
    cnt_ref[...] = run_ref[...]

    meta = jnp.where(lane == 0, g1, jnp.where(lane == 1, g2, 0.0))
    meta = jnp.where(lane == 2, r1, jnp.where(lane == 3, r2, meta))
    meta_ref[...] = jnp.where(lane == 4, i1, jnp.where(lane == 5, i2, meta))


def router(h, w_pad, *, tm):
    t, d = h.shape
    return pl.pallas_call(
        _router_kernel,
        grid=(t // tm,),
        in_specs=[
            pl.BlockSpec((tm, d), lambda i: (i, 0)),
            pl.BlockSpec((d, LANES), lambda i: (0, 0)),
        ],
        out_specs=[
            pl.BlockSpec((tm, LANES), lambda i: (i, 0)),
            pl.BlockSpec((1, LANES), lambda i: (0, 0)),
        ],
        out_shape=[
            jax.ShapeDtypeStruct((t, LANES), F32),
            jax.ShapeDtypeStruct((1, LANES), F32),
        ],
        scratch_shapes=[pltpu.VMEM((1, LANES), F32)],
        compiler_params=_params("arbitrary"),
        name="moe_router",
    )(h, w_pad)


def _dispatch_kernel(pad_ref, dest_hbm, h_hbm, xb_hbm, idx_smem, zrow, isem, sem, *, td):
    i = pl.program_id(0)
    cp = pltpu.make_async_copy(dest_hbm.at[i], idx_smem, isem)
    cp.start()
    cp.wait()

    def body(r, c):
        tok = i * td + r
        for k in range(2):
            pltpu.make_async_copy(h_hbm.at[pl.ds(tok, 1)], xb_hbm.at[pl.ds(idx_smem[2 * r + k], 1)], sem).start()
        return c

    lax.fori_loop(0, td, body, 0)

    @pl.when(i == 0)
    def _():
        zrow[...] = jnp.zeros(zrow.shape, zrow.dtype)
        nrange = pad_ref.shape[0] // 2
        for e in range(nrange):
            start = pad_ref[e]
            n = pad_ref[nrange + e]

            def zbody(r, c):
                pltpu.make_async_copy(zrow, xb_hbm.at[pl.ds(start + r, 1)], sem).start()
                return c

            lax.fori_loop(0, n, zbody, 0)

            def wbody(r, c):
                pltpu.make_async_copy(zrow, xb_hbm.at[pl.ds(0, 1)], sem).wait()
                return c

            lax.fori_loop(0, n, wbody, 0)

    pltpu.make_async_copy(h_hbm.at[pl.ds(0, 2 * td)], xb_hbm.at[pl.ds(0, 2 * td)], sem).wait()


def moe_dispatch(h, dest, pad_info, *, cap, td):
    t, d = h.shape
    return pl.pallas_call(
        functools.partial(_dispatch_kernel, td=td),
        grid_spec=pltpu.PrefetchScalarGridSpec(
            num_scalar_prefetch=1,
            grid=(t // td,),
            in_specs=[
                pl.BlockSpec(memory_space=pl.ANY),
                pl.BlockSpec(memory_space=pl.ANY),
            ],
            out_specs=pl.BlockSpec(memory_space=pl.ANY),
            scratch_shapes=[
                pltpu.SMEM((2 * td,), I32),
                pltpu.VMEM((1, d), h.dtype),
                pltpu.SemaphoreType.DMA,
                pltpu.SemaphoreType.DMA,
            ],
        ),
        out_shape=jax.ShapeDtypeStruct((cap, d), h.dtype),
        compiler_params=_params("arbitrary"),
        name="moe_dispatch",
    )(pad_info, dest, h)


def _experts_kernel(be_ref, x_ref, wg_ref, wu_ref, wd_ref, o_ref, xb_ref):
    i = pl.program_id(0)
    j = pl.program_id(1)
    used = i < be_ref[be_ref.shape[0] - 1]

    @pl.when(j == 0)
    def _():
        o_ref[...] = jnp.zeros(o_ref.shape, o_ref.dtype)

    @pl.when(used & (j == 0))
    def _():
        xb_ref[...] = x_ref[...].astype(BF16)

    @pl.when(used)
    def _():
        x = xb_ref[...]
        g = _dot(x, wg_ref[0])
        u = _dot(x, wu_ref[0])
        o_ref[...] += _dot((_silu(g) * u).astype(BF16), wd_ref[0])


def moe_experts(block_e, xb, wg, wu, wd, *, bm, tf):
    cap, d = xb.shape
    f = wg.shape[2]
    nblk = cap // bm
    return pl.pallas_call(
        _experts_kernel,
        grid_spec=pltpu.PrefetchScalarGridSpec(
            num_scalar_prefetch=1,
            grid=(nblk, f // tf),
            in_specs=[
                pl.BlockSpec((bm, d), lambda i, j, be: (jnp.minimum(i, be[nblk] - 1), 0)),
                pl.BlockSpec((1, d, tf), lambda i, j, be: (be[i], 0, j)),
                pl.BlockSpec((1, d, tf), lambda i, j, be: (be[i], 0, j)),
                pl.BlockSpec((1, tf, d), lambda i, j, be: (be[i], j, 0)),
            ],
            out_specs=pl.BlockSpec((bm, d), lambda i, j, be: (i, 0)),
            scratch_shapes=[pltpu.VMEM((bm, d), BF16)],
        ),
        out_shape=jax.ShapeDtypeStruct((cap, d), F32),
        compiler_params=_params("parallel", "arbitrary"),
        name="moe_experts",
    )(block_e, xb, wg, wu, wd)


def _combine_kernel(dest_hbm, yb_hbm, x_ref, gate_ref, o_ref, idx_smem, y0, y1, isem, sem, *, tc):
    i = pl.program_id(0)
    cp = pltpu.make_async_copy(dest_hbm.at[i], idx_smem, isem)
    cp.start()
    cp.wait()

    def body(r, c):
        pltpu.make_async_copy(yb_hbm.at[pl.ds(idx_smem[2 * r], 1)], y0.at[pl.ds(r, 1)], sem).start()
        pltpu.make_async_copy(yb_hbm.at[pl.ds(idx_smem[2 * r + 1], 1)], y1.at[pl.ds(r, 1)], sem).start()
        return c

    lax.fori_loop(0, tc, body, 0)
    pltpu.make_async_copy(yb_hbm.at[pl.ds(0, tc)], y0, sem).wait()
    pltpu.make_async_copy(yb_hbm.at[pl.ds(0, tc)], y1, sem).wait()
    gate = gate_ref[...]
    o_ref[...] = x_ref[...] + (gate[:, 0:1] * y0[...] + gate[:, 1:2] * y1[...])


def moe_combine(dest, yb, x, gates, *, tc):
    t, d = x.shape
    return pl.pallas_call(
        functools.partial(_combine_kernel, tc=tc),
        grid=(t // tc,),
        in_specs=[
            pl.BlockSpec(memory_space=pl.ANY),
            pl.BlockSpec(memory_space=pl.ANY),
            pl.BlockSpec((tc, d), lambda i: (i, 0)),
            pl.BlockSpec((tc, LANES), lambda i: (i, 0)),
        ],
        out_specs=pl.BlockSpec((tc, d), lambda i: (i, 0)),
        out_shape=jax.ShapeDtypeStruct((t, d), F32),
        scratch_shapes=[
            pltpu.SMEM((2 * tc,), I32),
            pltpu.VMEM((tc, d), F32),
            pltpu.VMEM((tc, d), F32),
            pltpu.SemaphoreType.DMA,
            pltpu.SemaphoreType.DMA,
        ],
        compiler_params=_params("arbitrary"),
        name="moe_combine",
    )(dest, yb, x, gates)


def _tile(n, pref):
    t = min(n, pref)
    while n % t:
        t //= 2
    return t


def even_layer(x, p, *, batch, seq):
    t, d = x.shape
    tm = _tile(t, 1024)
    proj, dt_raw = norm_proj(x, p["norm_mix"], p["w_in_main"], p["w_in_dt"], tm=tm, tn=p["tn_in"])
    mix = conv_ssd_mixer(proj, dt_raw, p["conv_w"], p["xw"], p["bcw"], p["xb"], p["bcb"], p["dt_bias"],
                         p["a_log"], p["d_skip"], p["ssd_norm_g"], p["expand"],
                         batch=batch, seq=seq, blk=_tile(seq, 256))
    x, h = proj_res_norm(mix, p["w_out"], x, p["norm_ffn"], tm=_tile(t, 512), h_dtype=BF16)
    return ffn(h, x, p["w_gate"], p["w_up"], p["w_down"], tm=tm, tf=256)


def moe_layer(x, h, p, *, bm):
    t, d = x.shape
    tr = _tile(t, 512)
    meta, counts = router(h, p["w_router"], tm=tr)
    rank = meta[:, 2:4].astype(I32)
    top_e = meta[:, 4:6].astype(I32)
    counts = counts[0, :N_EXPERTS].astype(I32)
    padded = (counts + bm - 1) // bm * bm
    pends = jnp.cumsum(padded)
    pstarts = pends - padded
    dest = pstarts[top_e] + rank
    nblk = (2 * t) // bm + N_EXPERTS
    cap = nblk * bm
    block_e = jnp.minimum(jnp.searchsorted(pends, jnp.arange(nblk, dtype=I32) * bm, side="right"),
                          N_EXPERTS - 1).astype(I32)
    block_e = jnp.concatenate([block_e, (pends[-1:] // bm).astype(I32)])
    pad_info = jnp.concatenate([pstarts + counts, pends[-1:], padded - counts, cap - pends[-1:]]).astype(I32)
    td = _tile(t, 512)
    dest_t = dest.reshape(t // td, 2 * td)
    xb = moe_dispatch(h, dest_t, pad_info, cap=cap, td=td)
    yb = moe_experts(block_e, xb, p["w_gate"], p["w_up"], p["w_down"], bm=bm, tf=256)
    return moe_combine(dest_t, yb, x, meta, tc=td)


def odd_layer(x, p, *, batch, seq, lambda_init, cos_t, sin_t):
    t, d = x.shape
    tm = _tile(seq, 1024)
    qkv = qkv_proj(x, p["norm_mix"], p["w_qkv"], p["head_gain"], cos_t, sin_t, p["pavg"], tm=tm, seq=seq)
    o = diff_attention(qkv, p["lam"], p["sub_norm"], batch=batch, seq=seq, heads=d // LANES,
                       tq=_tile(seq, 256), lambda_init=lambda_init)
    x, h = proj_res_norm(o, p["w_o"], x, p["norm_ffn"], tm=_tile(t, 512), h_dtype=F32)
    return moe_layer(x, h, p, bm=_tile(t, 512))


def _rope_tables(seq):
    inv = 1.0 / (ROPE_THETA ** (jnp.arange(0, HEAD, 2, dtype=F32) / HEAD))
    ang = jnp.arange(seq, dtype=F32)[:, None] * inv[None, :]
    cos, sin = jnp.cos(ang), jnp.sin(ang)
    cos_t = jnp.concatenate([cos, cos, cos, cos], axis=1)
    sin_t = jnp.concatenate([-sin, sin, -sin, sin], axis=1)
    return cos_t, sin_t


def kernel(x, even_norm_mix, even_w_in, even_conv_w, ssd_conv_w, ssd_conv_b, ssd_dt_bias, ssd_a_log, ssd_d, ssd_norm_g, even_w_out, even_norm_ffn, ffn_w_gate, ffn_w_up, ffn_w_down, odd_norm_mix, attn_w_qkv, attn_q_norm, attn_k_norm, attn_lambda_q1, attn_lambda_k1, attn_lambda_q2, attn_lambda_k2, attn_sub_norm, attn_w_o, odd_norm_ffn, moe_w_router, moe_w_gate, moe_w_up, moe_w_down):
    batch, seq, d = x.shape
    depth = even_norm_mix.shape[0] + odd_norm_mix.shape[0]
    n_main = 5 * d + 2 * LANES
    heads = jnp.arange(d) // HEAD
    expand = (jnp.arange(LANES)[:, None] == heads[None, :]).astype(BF16)
    pavg = jnp.where(heads[:, None] == heads[None, :], 1.0 / HEAD, 0.0).astype(BF16)
    cos_t, sin_t = _rope_tables(seq)
    scale = 1.0 / math.sqrt(HEAD)

    def pad_lanes(v):
        return jnp.pad(v.astype(F32), (0, LANES - v.shape[0]))[None, :]

    xf = x.reshape(batch * seq, d)
    for layer in range(depth):
        i = layer // 2
        if layer % 2 == 0:
            w_in = even_w_in[i]
            p = dict(
                norm_mix=even_norm_mix[i][None, :],
                w_in_main=w_in[:, :n_main].astype(BF16),
                w_in_dt=jnp.pad(w_in[:, n_main:], ((0, 0), (0, LANES - (w_in.shape[1] - n_main)))).astype(BF16),
                tn_in=n_main // 6,
                conv_w=even_conv_w[i],
                xw=ssd_conv_w[i][:, :d], bcw=ssd_conv_w[i][:, d:],
                xb=ssd_conv_b[i][None, :d], bcb=ssd_conv_b[i][None, d:],
                dt_bias=pad_lanes(ssd_dt_bias[i]), a_log=pad_lanes(ssd_a_log[i]),
                d_skip=jnp.repeat(ssd_d[i].astype(F32), HEAD)[None, :],
                ssd_norm_g=ssd_norm_g[i][None, :], expand=expand,
                w_out=even_w_out[i].astype(BF16), norm_ffn=even_norm_ffn[i][None, :],
                w_gate=ffn_w_gate[i].astype(BF16), w_up=ffn_w_up[i].astype(BF16),
                w_down=ffn_w_down[i].astype(BF16),
            )
            xf = even_layer(xf, p, batch=batch, seq=seq)
        else:
            lambda_init = 0.8 - 0.6 * math.exp(-0.3 * layer)
            reps = d // HEAD
            head_gain = jnp.stack([jnp.tile(attn_q_norm[i], reps) * scale, jnp.tile(attn_k_norm[i], reps)])[:, None, :]
            lam = jnp.stack([pad_lanes(v)[0] for v in
                             (attn_lambda_q1[i], attn_lambda_k1[i], attn_lambda_q2[i], attn_lambda_k2[i])])
            p = dict(
                norm_mix=odd_norm_mix[i][None, :], w_qkv=attn_w_qkv[i].astype(BF16),
                head_gain=head_gain.astype(F32), pavg=pavg, lam=lam,
                sub_norm=attn_sub_norm[i][None, :], w_o=attn_w_o[i].astype(BF16),
                norm_ffn=odd_norm_ffn[i][None, :],
                w_router=jnp.pad(moe_w_router[i], ((0, 0), (0, LANES - N_EXPERTS))),
                w_gate=moe_w_gate[i].astype(BF16), w_up=moe_w_up[i].astype(BF16),
                w_down=moe_w_down[i].astype(BF16),
            )
            xf = odd_layer(xf, p, batch=batch, seq=seq, lambda_init=lambda_init, cos_t=cos_t, sin_t=sin_t)
    return xf.reshape(batch, seq, d)
```

```python
import functools
import math

import jax
import jax.numpy as jnp
from jax import lax
from jax.experimental import pallas as pl
from jax.experimental.pallas import tpu as pltpu

F32 = jnp.float32
BF16 = jnp.bfloat16
I32 = jnp.int32

EPS = 1e-6
ROPE_THETA = 10000.0
CHUNK = 64
HEAD = 64
LANES = 128
SSD_HEADS = 16
SSD_GROUPS = 2
N_EXPERTS = 8
FF_TILE = 1408
VMEM_LIMIT = 48 * 1024 * 1024
NEG = -1e30


def _params(*sem):
    return pltpu.CompilerParams(dimension_semantics=sem, vmem_limit_bytes=VMEM_LIMIT)


def _rms(x, g):
    ms = jnp.mean(x * x, axis=-1, keepdims=True)
    return x * lax.rsqrt(ms + EPS) * g


def _silu(x):
    return x * (1.0 / (1.0 + jnp.exp(-x)))


def _dot(a, b):
    return jnp.dot(a, b, preferred_element_type=F32)


def _dot_nt(a, b):
    return lax.dot_general(a, b, (((1,), (1,)), ((), ())), preferred_element_type=F32)


def _dot_tn(a, b):
    return lax.dot_general(a, b, (((0,), (0,)), ((), ())), preferred_element_type=F32)


def _norm_proj_kernel(x_ref, g_ref, w_ref, wdt_ref, o_ref, odt_ref, h_ref):
    @pl.when(pl.program_id(1) == 0)
    def _():
        h = _rms(x_ref[...], g_ref[...]).astype(BF16)
        h_ref[...] = h
        odt_ref[...] = _dot(h, wdt_ref[...])

    o_ref[...] = _dot(h_ref[...], w_ref[...]).astype(o_ref.dtype)


def norm_proj(x, g, w_main, w_dt, *, tm, tn):
    t, d = x.shape
    n = w_main.shape[1]
    return pl.pallas_call(
        _norm_proj_kernel,
        grid=(t // tm, n // tn),
        in_specs=[
            pl.BlockSpec((tm, d), lambda i, j: (i, 0)),
            pl.BlockSpec((1, d), lambda i, j: (0, 0)),
            pl.BlockSpec((d, tn), lambda i, j: (0, j)),
            pl.BlockSpec((d, LANES), lambda i, j: (0, 0)),
        ],
        out_specs=[
            pl.BlockSpec((tm, tn), lambda i, j: (i, j)),
            pl.BlockSpec((tm, LANES), lambda i, j: (i, 0)),
        ],
        out_shape=[
            jax.ShapeDtypeStruct((t, n), BF16),
            jax.ShapeDtypeStruct((t, LANES), F32),
        ],
        scratch_shapes=[pltpu.VMEM((tm, d), BF16)],
        compiler_params=_params("parallel", "arbitrary"),
        name="norm_proj",
    )(x, g, w_main, w_dt)


HALO = 8


def _split3(v):
    hi = v.astype(BF16)
    r = v - hi.astype(F32)
    mid = r.astype(BF16)
    lo = (r - mid.astype(F32)).astype(BF16)
    return hi, mid, lo


def _expand(v, e, terms):
    parts = _split3(v)[:terms]
    out = _dot(parts[0], e)
    for p in parts[1:]:
        out = out + _dot(p, e)
    return out


def _mixer_kernel(bg_ref, cg_ref, hv_ref, z_ref, xs_ref, bc_ref, dtr_ref,
                  cw_ref, xw_ref, bcw_ref, xb_ref, bcb_ref, dtb_ref, alog_ref,
                  dskip_ref, ng_ref, e_ref,
                  o_ref, ubuf, xbuf, bcbuf, state, ybuf):
    L = o_ref.shape[0]
    d = bg_ref.shape[1]
    nchunk = L // CHUNK

    @pl.when(pl.program_id(1) == 0)
    def _():
        ubuf[0:HALO, :] = jnp.zeros((HALO, ubuf.shape[1]), F32)
        xbuf[0:HALO, :] = jnp.zeros((HALO, xbuf.shape[1]), F32)
        bcbuf[0:HALO, :] = jnp.zeros((HALO, bcbuf.shape[1]), F32)
        state[...] = jnp.zeros(state.shape, F32)

    u = cg_ref[...].astype(F32) * hv_ref[...].astype(F32)
    ubuf[HALO:HALO + L, :] = u
    cw = cw_ref[...]
    conv = (cw[0:1] * ubuf[HALO - 2:HALO - 2 + L, :]
            + cw[1:2] * ubuf[HALO - 1:HALO - 1 + L, :]
            + cw[2:3] * u)
    o_ref[:, 0:d] = (bg_ref[...].astype(F32) * conv).astype(o_ref.dtype)
    ubuf[0:HALO, :] = ubuf[L:L + HALO, :]

    def conv4(src_ref, buf, w_ref, b_ref):
        buf[HALO:HALO + L, :] = src_ref[...].astype(F32)
        w = w_ref[...]
        acc = b_ref[...] + w[3:4] * buf[HALO:HALO + L, :]
        for k in range(3):
            acc = acc + w[k:k + 1] * buf[HALO - 3 + k:HALO - 3 + k + L, :]
        buf[0:HALO, :] = buf[L:L + HALO, :]
        return _silu(acc)

    xc = conv4(xs_ref, xbuf, xw_ref, xb_ref)
    bcc = conv4(bc_ref, bcbuf, bcw_ref, bcb_ref)

    e = e_ref[...]
    dtv = dtr_ref[...] + dtb_ref[...]
    dt = jnp.maximum(dtv, 0.0) + jnp.log(1.0 + jnp.exp(-jnp.abs(dtv)))
    a = -jnp.exp(alog_ref[...])
    da = dt * a

    ri = lax.broadcasted_iota(I32, (L, L), 0)
    ci = lax.broadcasted_iota(I32, (L, L), 1)
    tri = jnp.where((ri >= ci) & ((ri // CHUNK) == (ci // CHUNK)), 1.0, 0.0).astype(BF16)
    d_hi, d_mid, d_lo = _split3(da)
    acs = _dot(tri, d_hi) + _dot(tri, d_mid) + _dot(tri, d_lo)

    last = jnp.concatenate(
        [jnp.broadcast_to(acs[c * CHUNK + CHUNK - 1:c * CHUNK + CHUNK, :], (CHUNK, LANES))
         for c in range(nchunk)], axis=0)
    col_all = _expand(acs, e, 3)
    e_all = jnp.exp(col_all)
    ds_all = jnp.exp(_expand(last, e, 3) - col_all)
    xdt = xc * _expand(dt, e, 3)
    xwt = (xdt * ds_all).astype(BF16)
    xdt_b = xdt.astype(BF16)

    lane = lax.broadcasted_iota(I32, (CHUNK, LANES), 1)
    row = lax.broadcasted_iota(I32, (CHUNK, LANES), 0)
    low_half = lane < HEAD
    tri2 = row >= (lane % HEAD)
    lane2 = lax.broadcasted_iota(I32, (2 * CHUNK, LANES), 1)
    row2 = lax.broadcasted_iota(I32, (2 * CHUNK, LANES), 0)
    bd_mask = (lane2 < HEAD) == (row2 < CHUNK)
    srow = lax.broadcasted_iota(I32, state.shape, 0)
    scol = lax.broadcasted_iota(I32, state.shape, 1)
    state_mask = (srow // HEAD) == (scol // (d // SSD_GROUPS))

    for c in range(nchunk):
        r0 = c * CHUNK
        bm = bcc[r0:r0 + CHUNK, 0:LANES]
        cm = bcc[r0:r0 + CHUNK, LANES:2 * LANES]
        bm_b = bm.astype(BF16)
        cm_b = cm.astype(BF16)
        acs_c = acs[r0:r0 + CHUNK, :]
        acs_t = jnp.concatenate([acs_c, acs_c], axis=0).T
        st = state[...]
        y_off = _dot(cm_b, st.astype(BF16)) * e_all[r0:r0 + CHUNK, :]
        bm2 = jnp.concatenate([bm_b, bm_b], axis=0)
        cb2 = []
        for g in range(SSD_GROUPS):
            cmm = jnp.where((lane // HEAD) == g, cm, 0.0).astype(BF16)
            cb2.append(_dot_nt(cmm, bm2))
        yd = []
        for q in range(SSD_HEADS // 2):
            colq = col_all[r0:r0 + CHUNK, q * LANES:(q + 1) * LANES]
            rowq = jnp.where(low_half[0:1], acs_t[2 * q:2 * q + 1, :], acs_t[2 * q + 1:2 * q + 2, :])
            dec = jnp.exp(jnp.where(tri2, colq - rowq, NEG))
            m2 = (cb2[q // (SSD_HEADS // 2 // SSD_GROUPS)] * dec).astype(BF16)
            xq = xdt_b[r0:r0 + CHUNK, q * LANES:(q + 1) * LANES]
            xq2 = jnp.concatenate([xq, xq], axis=0)
            xbd = jnp.where(bd_mask, xq2, jnp.zeros_like(xq2))
            yd.append(_dot(m2, xbd))
        y = jnp.concatenate(yd, axis=1) + y_off + dskip_ref[...] * xc[r0:r0 + CHUNK, :]
        ybuf[r0:r0 + CHUNK, :] = y
        s_new = _dot_tn(bm_b, xwt[r0:r0 + CHUNK, :])
        state[...] = st * e_all[r0 + CHUNK - 1:r0 + CHUNK, :] + jnp.where(state_mask, s_new, 0.0)

    y = ybuf[...] * _silu(z_ref[...].astype(F32))
    half = d // SSD_GROUPS
    ng = ng_ref[...]
    for g in range(SSD_GROUPS):
        yg = y[:, g * half:(g + 1) * half]
        o_ref[:, d + g * half:d + (g + 1) * half] = _rms(yg, ng[:, g * half:(g + 1) * half]).astype(o_ref.dtype)


def conv_ssd_mixer(proj, dt_raw, cw, xw, bcw, xb, bcb, dtb, alog, dskip, ng, expand, *, batch, seq, blk):
    t = proj.shape[0]
    d = 1024
    nb = seq // blk
    row = lambda b, s: b * nb + s
    full = lambda shape: pl.BlockSpec(shape, lambda b, s: (0, 0))
    return pl.pallas_call(
        _mixer_kernel,
        grid=(batch, nb),
        in_specs=[
            pl.BlockSpec((blk, d), lambda b, s: (row(b, s), 0)),
            pl.BlockSpec((blk, d), lambda b, s: (row(b, s), 1)),
            pl.BlockSpec((blk, d), lambda b, s: (row(b, s), 2)),
            pl.BlockSpec((blk, d), lambda b, s: (row(b, s), 3)),
            pl.BlockSpec((blk, d), lambda b, s: (row(b, s), 4)),
            pl.BlockSpec((blk, 2 * LANES), lambda b, s: (row(b, s), 5 * d // (2 * LANES))),
            pl.BlockSpec((blk, LANES), lambda b, s: (row(b, s), 0)),
            full(cw.shape), full(xw.shape), full(bcw.shape), full(xb.shape), full(bcb.shape),
            full(dtb.shape), full(alog.shape), full(dskip.shape), full(ng.shape), full(expand.shape),
        ],
        out_specs=pl.BlockSpec((blk, 2 * d), lambda b, s: (row(b, s), 0)),
        out_shape=jax.ShapeDtypeStruct((t, 2 * d), BF16),
        scratch_shapes=[
            pltpu.VMEM((blk + HALO, d), F32),
            pltpu.VMEM((blk + HALO, d), F32),
            pltpu.VMEM((blk + HALO, 2 * LANES), F32),
            pltpu.VMEM((SSD_GROUPS * HEAD, d), F32),
            pltpu.VMEM((blk, d), F32),
        ],
        compiler_params=_params("parallel", "arbitrary"),
        name="conv_ssd_mixer",
    )(proj, proj, proj, proj, proj, proj, dt_raw, cw, xw, bcw, xb, bcb, dtb, alog, dskip, ng, expand)


def _proj_res_norm_kernel(a_ref, w_ref, x_ref, g_ref, xo_ref, ho_ref):
    y = x_ref[...] + _dot(a_ref[...], w_ref[...])
    xo_ref[...] = y
    ho_ref[...] = _rms(y, g_ref[...]).astype(ho_ref.dtype)


def proj_res_norm(a, w, x, g, *, tm, h_dtype):
    t, k = a.shape
    d = w.shape[1]
    return pl.pallas_call(
        _proj_res_norm_kernel,
        grid=(t // tm,),
        in_specs=[
            pl.BlockSpec((tm, k), lambda i: (i, 0)),
            pl.BlockSpec((k, d), lambda i: (0, 0)),
            pl.BlockSpec((tm, d), lambda i: (i, 0)),
            pl.BlockSpec((1, d), lambda i: (0, 0)),
        ],
        out_specs=[
            pl.BlockSpec((tm, d), lambda i: (i, 0)),
            pl.BlockSpec((tm, d), lambda i: (i, 0)),
        ],
        out_shape=[
            jax.ShapeDtypeStruct((t, d), F32),
            jax.ShapeDtypeStruct((t, d), h_dtype),
        ],
        compiler_params=_params("parallel"),
        name="proj_res_norm",
    )(a, w, x, g)


def _ffn_kernel(h_ref, x_ref, wg_ref, wu_ref, wd_ref, o_ref):
    @pl.when(pl.program_id(1) == 0)
    def _():
        o_ref[...] = x_ref[...]

    h = h_ref[...]
    g = _dot(h, wg_ref[...])
    u = _dot(h, wu_ref[...])
    o_ref[...] += _dot((_silu(g) * u).astype(BF16), wd_ref[...])


def ffn(h, x, wg, wu, wd, *, tm, tf):
    t, d = x.shape
    f = wg.shape[1]
    return pl.pallas_call(
        _ffn_kernel,
        grid=(t // tm, f // tf),
        in_specs=[
            pl.BlockSpec((tm, d), lambda i, j: (i, 0)),
            pl.BlockSpec((tm, d), lambda i, j: (i, 0)),
            pl.BlockSpec((d, tf), lambda i, j: (0, j)),
            pl.BlockSpec((d, tf), lambda i, j: (0, j)),
            pl.BlockSpec((tf, d), lambda i, j: (j, 0)),
        ],
        out_specs=pl.BlockSpec((tm, d), lambda i, j: (i, 0)),
        out_shape=jax.ShapeDtypeStruct((t, d), F32),
        compiler_params=_params("parallel", "arbitrary"),
        name="ffn",
    )(h, x, wg, wu, wd)


def _qkv_kernel(x_ref, g_ref, w_ref, hg_ref, cos_ref, sin_ref, p_ref, o_ref, h_ref):
    j = pl.program_id(1)

    @pl.when(j == 0)
    def _():
        h_ref[...] = _rms(x_ref[...], g_ref[...]).astype(BF16)

    y = _dot(h_ref[...], w_ref[...])

    @pl.when(j < 2)
    def _():
        d = y.shape[1]
        ms = _dot((y * y).astype(BF16), p_ref[...])
        yn = y * lax.rsqrt(ms + EPS) * hg_ref[0]
        lane = lax.broadcasted_iota(I32, y.shape, 1)
        partner = jnp.where((lane % HEAD) < HEAD // 2,
                            pltpu.roll(yn, d - HEAD // 2, 1), pltpu.roll(yn, HEAD // 2, 1))
        reps = d // LANES
        cos = jnp.concatenate([cos_ref[...]] * reps, axis=1)
        sin = jnp.concatenate([sin_ref[...]] * reps, axis=1)
        o_ref[...] = (yn * cos + partner * sin).astype(o_ref.dtype)

    @pl.when(j == 2)
    def _():
        o_ref[...] = y.astype(o_ref.dtype)


def qkv_proj(x, g, w, head_gain, cos_t, sin_t, pavg, *, tm, seq):
    t, d = x.shape
    ns = seq // tm
    return pl.pallas_call(
        _qkv_kernel,
        grid=(t // tm, 3),
        in_specs=[
            pl.BlockSpec((tm, d), lambda i, j: (i, 0)),
            pl.BlockSpec((1, d), lambda i, j: (0, 0)),
            pl.BlockSpec((d, d), lambda i, j: (0, j)),
            pl.BlockSpec((1, 1, d), lambda i, j: (jnp.minimum(j, 1), 0, 0)),
            pl.BlockSpec((tm, LANES), lambda i, j: (i % ns, 0)),
            pl.BlockSpec((tm, LANES), lambda i, j: (i % ns, 0)),
            pl.BlockSpec((d, d), lambda i, j: (0, 0)),
        ],
        out_specs=pl.BlockSpec((tm, d), lambda i, j: (i, j)),
        out_shape=jax.ShapeDtypeStruct((t, 3 * d), BF16),
        scratch_shapes=[pltpu.VMEM((tm, d), BF16)],
        compiler_params=_params("parallel", "arbitrary"),
        name="qkv_proj",
    )(x, g, w, head_gain, cos_t, sin_t, pavg)


def _attn_kernel(q_ref, k_ref, v_ref, lam_ref, sn_ref, o_ref, m_ref, l_ref, acc_ref, *, lambda_init):
    tq = q_ref.shape[0]
    i = pl.program_id(2)
    q = q_ref[...]
    lane = lax.broadcasted_iota(I32, q.shape, 1)
    zero = jnp.zeros_like(q)
    q2 = jnp.concatenate([jnp.where(lane < HEAD, q, zero), jnp.where(lane >= HEAD, q, zero)], axis=0)

    m_ref[...] = jnp.full(m_ref.shape, NEG, F32)
    l_ref[...] = jnp.zeros(l_ref.shape, F32)
    acc_ref[...] = jnp.zeros(acc_ref.shape, F32)

    def update(start, size, masked):
        k = k_ref[pl.ds(start, size), :]
        v = v_ref[pl.ds(start, size), :]
        s = _dot_nt(q2, k)
        if masked:
            qi = lax.broadcasted_iota(I32, s.shape, 0) % tq
            ki = lax.broadcasted_iota(I32, s.shape, 1)
            s = jnp.where((ki // CHUNK) <= (qi // CHUNK), s, NEG)
        m_old = m_ref[...]
        m_new = jnp.maximum(m_old, jnp.max(s, axis=-1, keepdims=True))
        alpha = jnp.exp2(m_old - m_new)
        p = jnp.exp2(s - jnp.tile(m_new, (1, size // LANES)))
        l_ref[...] = alpha * l_ref[...] + jnp.sum(p, axis=-1, keepdims=True)
        acc_ref[...] = alpha * acc_ref[...] + _dot(p.astype(BF16), v)
        m_ref[...] = m_new

    def body(kb, c):
        update(pl.multiple_of(kb * (2 * tq), 2 * tq), 2 * tq, False)
        return c

    lax.fori_loop(0, i // 2, body, 0)

    @pl.when(i % 2 == 1)
    def _():
        update(pl.multiple_of((i - 1) * tq, tq), tq, False)

    update(pl.multiple_of(i * tq, tq), tq, True)

    lam_p = lam_ref[...]
    lam = (jnp.exp(jnp.sum(lam_p[0:1] * lam_p[1:2], axis=-1, keepdims=True))
           - jnp.exp(jnp.sum(lam_p[2:3] * lam_p[3:4], axis=-1, keepdims=True)) + lambda_init)
    o = acc_ref[...] / l_ref[...]
    o = o[0:tq] - lam * o[tq:2 * tq]
    o_ref[...] = (_rms(o, sn_ref[...]) * (1.0 - lambda_init)).astype(o_ref.dtype)


def diff_attention(qkv, lam_p, sub_norm, *, batch, seq, heads, tq, lambda_init):
    t = qkv.shape[0]
    nq = seq // tq
    return pl.pallas_call(
        functools.partial(_attn_kernel, lambda_init=lambda_init),
        grid=(batch, heads, nq),
        in_specs=[
            pl.BlockSpec((tq, LANES), lambda b, h, i: (b * nq + i, h)),
            pl.BlockSpec((seq, LANES), lambda b, h, i: (b, heads + h)),
            pl.BlockSpec((seq, LANES), lambda b, h, i: (b, 2 * heads + h)),
            pl.BlockSpec(lam_p.shape, lambda b, h, i: (0, 0)),
            pl.BlockSpec(sub_norm.shape, lambda b, h, i: (0, 0)),
        ],
        out_specs=pl.BlockSpec((tq, LANES), lambda b, h, i: (b * nq + i, h)),
        out_shape=jax.ShapeDtypeStruct((t, heads * LANES), BF16),
        scratch_shapes=[
            pltpu.VMEM((2 * tq, LANES), F32),
            pltpu.VMEM((2 * tq, LANES), F32),
            pltpu.VMEM((2 * tq, LANES), F32),
        ],
        compiler_params=_params("parallel", "parallel", "arbitrary"),
        name="diff_attention",
    )(qkv, qkv, qkv, lam_p, sub_norm)


def _router_kernel(h_ref, w_ref, meta_ref, cnt_ref, run_ref):
    tm = h_ref.shape[0]

    @pl.when(pl.program_id(0) == 0)
    def _():
        run_ref[...] = jnp.zeros(run_ref.shape, F32)

    h = h_ref[...]
    w = w_ref[...]
    h_hi = h.astype(BF16)
    h_lo = (h - h_hi.astype(F32)).astype(BF16)
    w_hi = w.astype(BF16)
    w_lo = (w - w_hi.astype(F32)).astype(BF16)
    logits = _dot(h_hi, w_hi) + (_dot(h_hi, w_lo) + _dot(h_lo, w_hi))
    lane = lax.broadcasted_iota(I32, logits.shape, 1)
    lane_f = lane.astype(F32)
    logits = jnp.where(lane < N_EXPERTS, logits, NEG)
    m1 = jnp.max(logits, axis=-1, keepdims=True)
    i1 = jnp.min(jnp.where(logits == m1, lane_f, float(LANES)), axis=-1, keepdims=True)
    rest = jnp.where(lane_f == i1, NEG, logits)
    m2 = jnp.max(rest, axis=-1, keepdims=True)
    i2 = jnp.min(jnp.where(rest == m2, lane_f, float(LANES)), axis=-1, keepdims=True)
    e2 = jnp.exp(m2 - m1)
    g1 = 1.0 / (1.0 + e2)
    g2 = e2 * g1

    oh1 = lane_f == i1
    oh2 = lane_f == i2
    both = jnp.where(oh1, 1.0, 0.0) + jnp.where(oh2, 1.0, 0.0)
    ri = lax.broadcasted_iota(I32, (tm, tm), 0)
    ci = lax.broadcasted_iota(I32, (tm, tm), 1)
    strict = jnp.where(ri > ci, 1.0, 0.0).astype(BF16)
    before = run_ref[...] + _dot(strict, both.astype(BF16))
    r1 = jnp.sum(jnp.where(oh1, before, 0.0), axis=-1, keepdims=True)
    r2 = jnp.sum(jnp.where(oh2, before, 0.0), axis=-1, keepdims=True)
    run_ref[...] += jnp.sum(both, axis=0, keepdims=True)
    cnt_ref[...] = run_ref[...]

    meta = jnp.where(lane == 0, g1, jnp.where(lane == 1, g2, 0.0))
    meta = jnp.where(lane == 2, r1, jnp.where(lane == 3, r2, meta))
    meta_ref[...] = jnp.where(lane == 4, i1, jnp.where(lane == 5, i2, meta))


def router(h, w_pad, *, tm):
    t, d = h.shape
    return pl.pallas_call(
        _router_kernel,
        grid=(t // tm,),
        in_specs=[
            pl.BlockSpec((tm, d), lambda i: (i, 0)),
            pl.BlockSpec((d, LANES), lambda i: (0, 0)),
        ],
        out_specs=[
            pl.BlockSpec((tm, LANES), lambda i: (i, 0)),
            pl.BlockSpec((1, LANES), lambda i: (0, 0)),
        ],
        out_shape=[
            jax.ShapeDtypeStruct((t, LANES), F32),
            jax.ShapeDtypeStruct((1, LANES), F32),
        ],
        scratch_shapes=[pltpu.VMEM((1, LANES), F32)],
        compiler_params=_params("arbitrary"),
        name="moe_router",
    )(h, w_pad)


def _dispatch_kernel(pad_ref, dest_hbm, h_ref, xb_hbm, idx_smem, zrow, isem, sem, *, td):
    i = pl.program_id(0)
    cp = pltpu.make_async_copy(dest_hbm.at[i], idx_smem, isem)
    cp.start()
    cp.wait()

    def body(r, c):
        for k in range(2):
            pltpu.make_async_copy(h_ref.at[pl.ds(r, 1)], xb_hbm.at[pl.ds(idx_smem[2 * r + k], 1)], sem).start()
        return c

    lax.fori_loop(0, td, body, 0, unroll=8)

    @pl.when(i == 0)
    def _():
        zrow[...] = jnp.zeros(zrow.shape, zrow.dtype)
        nrange = pad_ref.shape[0] // 2
        for e in range(nrange):
            start = pad_ref[e]
            n = pad_ref[nrange + e]

            def zbody(r, c):
                pltpu.make_async_copy(zrow, xb_hbm.at[pl.ds(start + r, 1)], sem).start()
                return c

            lax.fori_loop(0, n, zbody, 0)

            def wbody(r, c):
                pltpu.make_async_copy(zrow, xb_hbm.at[pl.ds(0, 1)], sem).wait()
                return c

            lax.fori_loop(0, n, wbody, 0)

    for k in range(2):
        pltpu.make_async_copy(h_ref, xb_hbm.at[pl.ds(0, td)], sem).wait()


def moe_dispatch(h, dest, pad_info, *, cap, td):
    t, d = h.shape
    return pl.pallas_call(
        functools.partial(_dispatch_kernel, td=td),
        grid_spec=pltpu.PrefetchScalarGridSpec(
            num_scalar_prefetch=1,
            grid=(t // td,),
            in_specs=[
                pl.BlockSpec(memory_space=pl.ANY),
                pl.BlockSpec((td, d), lambda i, pad: (i, 0)),
            ],
            out_specs=pl.BlockSpec(memory_space=pl.ANY),
            scratch_shapes=[
                pltpu.SMEM((2 * td,), I32),
                pltpu.VMEM((1, d), h.dtype),
                pltpu.SemaphoreType.DMA,
                pltpu.SemaphoreType.DMA,
            ],
        ),
        out_shape=jax.ShapeDtypeStruct((cap, d), h.dtype),
        compiler_params=_params("arbitrary"),
        name="moe_dispatch",
    )(pad_info, dest, h)


def _experts_kernel(be_ref, x_ref, wg_ref, wu_ref, wd_ref, o_ref, xb_ref):
    i = pl.program_id(0)
    j = pl.program_id(1)
    used = i < be_ref[be_ref.shape[0] - 1]

    @pl.when(j == 0)
    def _():
        o_ref[...] = jnp.zeros(o_ref.shape, o_ref.dtype)

    @pl.when(used & (j == 0))
    def _():
        xb_ref[...] = x_ref[...].astype(BF16)

    @pl.when(used)
    def _():
        x = xb_ref[...]
        g = _dot(x, wg_ref[0])
        u = _dot(x, wu_ref[0])
        o_ref[...] += _dot((_silu(g) * u).astype(BF16), wd_ref[0])


def moe_experts(block_e, xb, wg, wu, wd, *, bm, tf):
    cap, d = xb.shape
    f = wg.shape[2]
    nblk = cap // bm
    return pl.pallas_call(
        _experts_kernel,
        grid_spec=pltpu.PrefetchScalarGridSpec(
            num_scalar_prefetch=1,
            grid=(nblk, f // tf),
            in_specs=[
                pl.BlockSpec((bm, d), lambda i, j, be: (jnp.minimum(i, be[nblk] - 1), 0)),
                pl.BlockSpec((1, d, tf), lambda i, j, be: (be[i], 0, j)),
                pl.BlockSpec((1, d, tf), lambda i, j, be: (be[i], 0, j)),
                pl.BlockSpec((1, tf, d), lambda i, j, be: (be[i], j, 0)),
            ],
            out_specs=pl.BlockSpec((bm, d), lambda i, j, be: (i, 0)),
            scratch_shapes=[pltpu.VMEM((bm, d), BF16)],
        ),
        out_shape=jax.ShapeDtypeStruct((cap, d), F32),
        compiler_params=_params("parallel", "arbitrary"),
        name="moe_experts",
    )(block_e, xb, wg, wu, wd)


def _combine_kernel(dest_hbm, yb_hbm, x_ref, gate_ref, o_ref, idx_smem, y0, y1, isem, sem, *, tc):
    i = pl.program_id(0)
    cp = pltpu.make_async_copy(dest_hbm.at[i], idx_smem, isem)
    cp.start()
    cp.wait()

    def body(r, c):
        pltpu.make_async_copy(yb_hbm.at[pl.ds(idx_smem[2 * r], 1)], y0.at[pl.ds(r, 1)], sem).start()
        pltpu.make_async_copy(yb_hbm.at[pl.ds(idx_smem[2 * r + 1], 1)], y1.at[pl.ds(r, 1)], sem).start()
        return c

    lax.fori_loop(0, tc, body, 0, unroll=8)
    pltpu.make_async_copy(yb_hbm.at[pl.ds(0, tc)], y0, sem).wait()
    pltpu.make_async_copy(yb_hbm.at[pl.ds(0, tc)], y1, sem).wait()
    gate = gate_ref[...]
    o_ref[...] = x_ref[...] + (gate[:, 0:1] * y0[...] + gate[:, 1:2] * y1[...])


def moe_combine(dest, yb, x, gates, *, tc):
    t, d = x.shape
    return pl.pallas_call(
        functools.partial(_combine_kernel, tc=tc),
        grid=(t // tc,),
        in_specs=[
            pl.BlockSpec(memory_space=pl.ANY),
            pl.BlockSpec(memory_space=pl.ANY),
            pl.BlockSpec((tc, d), lambda i: (i, 0)),
            pl.BlockSpec((tc, LANES), lambda i: (i, 0)),
        ],
        out_specs=pl.BlockSpec((tc, d), lambda i: (i, 0)),
        out_shape=jax.ShapeDtypeStruct((t, d), F32),
        scratch_shapes=[
            pltpu.SMEM((2 * tc,), I32),
            pltpu.VMEM((tc, d), F32),
            pltpu.VMEM((tc, d), F32),
            pltpu.SemaphoreType.DMA,
            pltpu.SemaphoreType.DMA,
        ],
        compiler_params=_params("arbitrary"),
        name="moe_combine",
    )(dest, yb, x, gates)


def _tile(n, pref):
    t = min(n, pref)
    while n % t:
        t //= 2
    return t


def even_layer(x, p, *, batch, seq):
    t, d = x.shape
    tm = _tile(t, 1024)
    proj, dt_raw = norm_proj(x, p["norm_mix"], p["w_in_main"], p["w_in_dt"], tm=tm, tn=p["tn_in"])
    mix = conv_ssd_mixer(proj, dt_raw, p["conv_w"], p["xw"], p["bcw"], p["xb"], p["bcb"], p["dt_bias"],
                         p["a_log"], p["d_skip"], p["ssd_norm_g"], p["expand"],
                         batch=batch, seq=seq, blk=_tile(seq, 256))
    x, h = proj_res_norm(mix, p["w_out"], x, p["norm_ffn"], tm=_tile(t, 512), h_dtype=BF16)
    return ffn(h, x, p["w_gate"], p["w_up"], p["w_down"], tm=_tile(t, 512), tf=FF_TILE)


def moe_layer(x, h, p, *, bm):
    t, d = x.shape
    tr = _tile(t, 512)
    meta, counts = router(h, p["w_router"], tm=tr)
    rank = meta[:, 2:4].astype(I32)
    top_e = meta[:, 4:6].astype(I32)
    counts = counts[0, :N_EXPERTS].astype(I32)
    padded = (counts + bm - 1) // bm * bm
    pends = jnp.cumsum(padded)
    pstarts = pends - padded
    dest = pstarts[top_e] + rank
    nblk = (2 * t) // bm + N_EXPERTS
    cap = nblk * bm
    block_e = jnp.minimum(jnp.searchsorted(pends, jnp.arange(nblk, dtype=I32) * bm, side="right"),
                          N_EXPERTS - 1).astype(I32)
    block_e = jnp.concatenate([block_e, (pends[-1:] // bm).astype(I32)])
    pad_info = jnp.concatenate([pstarts + counts, pends[-1:], padded - counts, cap - pends[-1:]]).astype(I32)
    td = _tile(t, 512)
    dest_t = dest.reshape(t // td, 2 * td)
    xb = moe_dispatch(h, dest_t, pad_info, cap=cap, td=td)
    yb = moe_experts(block_e, xb, p["w_gate"], p["w_up"], p["w_down"], bm=bm, tf=FF_TILE)
    return moe_combine(dest_t, yb, x, meta, tc=td)


def odd_layer(x, p, *, batch, seq, lambda_init, cos_t, sin_t):
    t, d = x.shape
    tm = _tile(seq, 1024)
    qkv = qkv_proj(x, p["norm_mix"], p["w_qkv"], p["head_gain"], cos_t, sin_t, p["pavg"], tm=tm, seq=seq)
    o = diff_attention(qkv, p["lam"], p["sub_norm"], batch=batch, seq=seq, heads=d // LANES,
                       tq=_tile(seq, 256), lambda_init=lambda_init)
    x, h = proj_res_norm(o, p["w_o"], x, p["norm_ffn"], tm=_tile(t, 512), h_dtype=F32)
    return moe_layer(x, h, p, bm=_tile(t, 512))


def _rope_tables(seq):
    inv = 1.0 / (ROPE_THETA ** (jnp.arange(0, HEAD, 2, dtype=F32) / HEAD))
    ang = jnp.arange(seq, dtype=F32)[:, None] * inv[None, :]
    cos, sin = jnp.cos(ang), jnp.sin(ang)
    cos_t = jnp.concatenate([cos, cos, cos, cos], axis=1)
    sin_t = jnp.concatenate([-sin, sin, -sin, sin], axis=1)
    return cos_t, sin_t


def kernel(x, even_norm_mix, even_w_in, even_conv_w, ssd_conv_w, ssd_conv_b, ssd_dt_bias, ssd_a_log, ssd_d, ssd_norm_g, even_w_out, even_norm_ffn, ffn_w_gate, ffn_w_up, ffn_w_down, odd_norm_mix, attn_w_qkv, attn_q_norm, attn_k_norm, attn_lambda_q1, attn_lambda_k1, attn_lambda_q2, attn_lambda_k2, attn_sub_norm, attn_w_o, odd_norm_ffn, moe_w_router, moe_w_gate, moe_w_up, moe_w_down):
    batch, seq, d = x.shape
    depth = even_norm_mix.shape[0] + odd_norm_mix.shape[0]
    n_main = 5 * d + 2 * LANES
    heads = jnp.arange(d) // HEAD
    expand = (jnp.arange(LANES)[:, None] == heads[None, :]).astype(BF16)
    pavg = jnp.where(heads[:, None] == heads[None, :], 1.0 / HEAD, 0.0).astype(BF16)
    cos_t, sin_t = _rope_tables(seq)
    scale = math.log2(math.e) / math.sqrt(HEAD)

    def pad_lanes(v):
        return jnp.pad(v.astype(F32), (0, LANES - v.shape[0]))[None, :]

    xf = x.reshape(batch * seq, d)
    for layer in range(depth):
        i = layer // 2
        if layer % 2 == 0:
            w_in = even_w_in[i]
            p = dict(
                norm_mix=even_norm_mix[i][None, :],
                w_in_main=w_in[:, :n_main].astype(BF16),
                w_in_dt=jnp.pad(w_in[:, n_main:], ((0, 0), (0, LANES - (w_in.shape[1] - n_main)))).astype(BF16),
                tn_in=n_main // 6,
                conv_w=even_conv_w[i],
                xw=ssd_conv_w[i][:, :d], bcw=ssd_conv_w[i][:, d:],
                xb=ssd_conv_b[i][None, :d], bcb=ssd_conv_b[i][None, d:],
                dt_bias=pad_lanes(ssd_dt_bias[i]), a_log=pad_lanes(ssd_a_log[i]),
                d_skip=jnp.repeat(ssd_d[i].astype(F32), HEAD)[None, :],
                ssd_norm_g=ssd_norm_g[i][None, :], expand=expand,
                w_out=even_w_out[i].astype(BF16), norm_ffn=even_norm_ffn[i][None, :],
                w_gate=ffn_w_gate[i].astype(BF16), w_up=ffn_w_up[i].astype(BF16),
                w_down=ffn_w_down[i].astype(BF16),
            )
            xf = even_layer(xf, p, batch=batch, seq=seq)
        else:
            lambda_init = 0.8 - 0.6 * math.exp(-0.3 * layer)
            reps = d // HEAD
            head_gain = jnp.stack([jnp.tile(attn_q_norm[i], reps) * scale, jnp.tile(attn_k_norm[i], reps)])[:, None, :]
            lam = jnp.stack([pad_lanes(v)[0] for v in
                             (attn_lambda_q1[i], attn_lambda_k1[i], attn_lambda_q2[i], attn_lambda_k2[i])])
            p = dict(
                norm_mix=odd_norm_mix[i][None, :], w_qkv=attn_w_qkv[i].astype(BF16),
                head_gain=head_gain.astype(F32), pavg=pavg, lam=lam,
                sub_norm=attn_sub_norm[i][None, :], w_o=attn_w_o[i].astype(BF16),
                norm_ffn=odd_norm_ffn[i][None, :],
                w_router=jnp.pad(moe_w_router[i], ((0, 0), (0, LANES - N_EXPERTS))),
                w_gate=moe_w_gate[i].astype(BF16), w_up=moe_w_up[i].astype(BF16),
                w_down=moe_w_down[i].astype(BF16),
            )
            xf = odd_layer(xf, p, batch=batch, seq=seq, lambda_init=lambda_init, cos_t=cos_t, sin_t=sin_t)
    return xf.reshape(batch, seq, d)
```

```python
import functools
import math

import jax
import jax.numpy as jnp
from jax import lax
from jax.experimental import pallas as pl
from jax.experimental.pallas import tpu as pltpu

F32 = jnp.float32
BF16 = jnp.bfloat16
I32 = jnp.int32

EPS = 1e-6
ROPE_THETA = 10000.0
CHUNK = 64
HEAD = 64
LANES = 128
SSD_HEADS = 16
SSD_GROUPS = 2
N_EXPERTS = 8
FF_TILE = 1408
VMEM_LIMIT = 56 * 1024 * 1024
NEG = -1e30


def _params(*sem):
    return pltpu.CompilerParams(dimension_semantics=sem, vmem_limit_bytes=VMEM_LIMIT)


def _rms(x, g):
    ms = jnp.mean(x * x, axis=-1, keepdims=True)
    return x * lax.rsqrt(ms + EPS) * g


def _silu(x):
    return x * (1.0 / (1.0 + jnp.exp(-x)))


def _dot(a, b):
    return jnp.dot(a, b, preferred_element_type=F32)


def _dot_nt(a, b):
    return lax.dot_general(a, b, (((1,), (1,)), ((), ())), preferred_element_type=F32)


def _dot_tn(a, b):
    return lax.dot_general(a, b, (((0,), (0,)), ((), ())), preferred_element_type=F32)


def _norm_proj_kernel(x_ref, g_ref, w_ref, wdt_ref, o_ref, odt_ref, h_ref):
    @pl.when(pl.program_id(1) == 0)
    def _():
        h = _rms(x_ref[...], g_ref[...]).astype(BF16)
        h_ref[...] = h
        odt_ref[...] = _dot(h, wdt_ref[...])

    o_ref[...] = _dot(h_ref[...], w_ref[...]).astype(o_ref.dtype)


def norm_proj(x, g, w_main, w_dt, *, tm, tn):
    t, d = x.shape
    n = w_main.shape[1]
    return pl.pallas_call(
        _norm_proj_kernel,
        grid=(t // tm, n // tn),
        in_specs=[
            pl.BlockSpec((tm, d), lambda i, j: (i, 0)),
            pl.BlockSpec((1, d), lambda i, j: (0, 0)),
            pl.BlockSpec((d, tn), lambda i, j: (0, j)),
            pl.BlockSpec((d, LANES), lambda i, j: (0, 0)),
        ],
        out_specs=[
            pl.BlockSpec((tm, tn), lambda i, j: (i, j)),
            pl.BlockSpec((tm, LANES), lambda i, j: (i, 0)),
        ],
        out_shape=[
            jax.ShapeDtypeStruct((t, n), BF16),
            jax.ShapeDtypeStruct((t, LANES), F32),
        ],
        scratch_shapes=[pltpu.VMEM((tm, d), BF16)],
        compiler_params=_params("parallel", "arbitrary"),
        name="norm_proj",
    )(x, g, w_main, w_dt)


HALO = 8


def _split3(v):
    hi = v.astype(BF16)
    r = v - hi.astype(F32)
    mid = r.astype(BF16)
    lo = (r - mid.astype(F32)).astype(BF16)
    return hi, mid, lo


def _expand(v, e, terms):
    parts = _split3(v)[:terms]
    out = _dot(parts[0], e)
    for p in parts[1:]:
        out = out + _dot(p, e)
    return out


def _mixer_kernel(bg_ref, cg_ref, hv_ref, z_ref, xs_ref, bc_ref, dtr_ref,
                  cw_ref, xw_ref, bcw_ref, xb_ref, bcb_ref, dtb_ref, alog_ref,
                  dskip_ref, ng_ref, e_ref,
                  o_ref, ubuf, xbuf, bcbuf, state, ybuf):
    L = o_ref.shape[0]
    d = bg_ref.shape[1]
    nchunk = L // CHUNK

    @pl.when(pl.program_id(1) == 0)
    def _():
        for buf in (ubuf, xbuf, bcbuf):
            buf[:, 0:HALO, :] = jnp.zeros((buf.shape[0], HALO, LANES), F32)
        state[...] = jnp.zeros(state.shape, F32)

    def causal_conv(v, buf, w):
        taps = w.shape[0]
        cols = []
        for c in range(v.shape[1] // LANES):
            lanes = slice(c * LANES, (c + 1) * LANES)
            vc = v[:, lanes]
            buf[c, HALO:HALO + L, :] = vc
            acc = w[taps - 1:taps, lanes] * vc
            for k in range(taps - 1):
                off = HALO - (taps - 1) + k
                acc = acc + w[k:k + 1, lanes] * buf[c, off:off + L, :]
            buf[c, 0:HALO, :] = buf[c, L:L + HALO, :]
            cols.append(acc)
        return jnp.concatenate(cols, axis=1)

    u = cg_ref[...].astype(F32) * hv_ref[...].astype(F32)
    conv = causal_conv(u, ubuf, cw_ref[...])
    o_ref[:, 0:d] = (bg_ref[...].astype(F32) * conv).astype(o_ref.dtype)

    xc = _silu(causal_conv(xs_ref[...].astype(F32), xbuf, xw_ref[...]) + xb_ref[...])
    bcc = _silu(causal_conv(bc_ref[...].astype(F32), bcbuf, bcw_ref[...]) + bcb_ref[...])

    e = e_ref[...]
    dtv = dtr_ref[...] + dtb_ref[...]
    dt = jnp.maximum(dtv, 0.0) + jnp.log(1.0 + jnp.exp(-jnp.abs(dtv)))
    a = -jnp.exp(alog_ref[...])
    da = dt * a

    ri = lax.broadcasted_iota(I32, (L, L), 0)
    ci = lax.broadcasted_iota(I32, (L, L), 1)
    tri = jnp.where((ri >= ci) & ((ri // CHUNK) == (ci // CHUNK)), 1.0, 0.0).astype(BF16)
    d_hi, d_mid, d_lo = _split3(da)
    acs = _dot(tri, d_hi) + _dot(tri, d_mid) + _dot(tri, d_lo)

    last = jnp.concatenate(
        [jnp.broadcast_to(acs[c * CHUNK + CHUNK - 1:c * CHUNK + CHUNK, :], (CHUNK, LANES))
         for c in range(nchunk)], axis=0)
    col_all = _expand(acs, e, 3)
    e_all = jnp.exp(col_all)
    ds_all = jnp.exp(_expand(last, e, 3) - col_all)
    xdt = xc * _expand(dt, e, 3)
    xwt = (xdt * ds_all).astype(BF16)
    xdt_b = xdt.astype(BF16)

    lane = lax.broadcasted_iota(I32, (CHUNK, LANES), 1)
    row = lax.broadcasted_iota(I32, (CHUNK, LANES), 0)
    low_half = lane < HEAD
    tri2 = row >= (lane % HEAD)
    lane2 = lax.broadcasted_iota(I32, (2 * CHUNK, LANES), 1)
    row2 = lax.broadcasted_iota(I32, (2 * CHUNK, LANES), 0)
    bd_mask = (lane2 < HEAD) == (row2 < CHUNK)
    srow = lax.broadcasted_iota(I32, state.shape, 0)
    scol = lax.broadcasted_iota(I32, state.shape, 1)
    state_mask = (srow // HEAD) == (scol // (d // SSD_GROUPS))

    for c in range(nchunk):
        r0 = c * CHUNK
        bm = bcc[r0:r0 + CHUNK, 0:LANES]
        cm = bcc[r0:r0 + CHUNK, LANES:2 * LANES]
        bm_b = bm.astype(BF16)
        cm_b = cm.astype(BF16)
        acs_c = acs[r0:r0 + CHUNK, :]
        acs_t = jnp.concatenate([acs_c, acs_c], axis=0).T
        st = state[...]
        y_off = _dot(cm_b, st.astype(BF16)) * e_all[r0:r0 + CHUNK, :]
        bm2 = jnp.concatenate([bm_b, bm_b], axis=0)
        cb2 = []
        for g in range(SSD_GROUPS):
            cmm = jnp.where((lane // HEAD) == g, cm, 0.0).astype(BF16)
            cb2.append(_dot_nt(cmm, bm2))
        yd = []
        for q in range(SSD_HEADS // 2):
            colq = col_all[r0:r0 + CHUNK, q * LANES:(q + 1) * LANES]
            rowq = jnp.where(low_half[0:1], acs_t[2 * q:2 * q + 1, :], acs_t[2 * q + 1:2 * q + 2, :])
            dec = jnp.exp(jnp.where(tri2, colq - rowq, NEG))
            m2 = (cb2[q // (SSD_HEADS // 2 // SSD_GROUPS)] * dec).astype(BF16)
            xq = xdt_b[r0:r0 + CHUNK, q * LANES:(q + 1) * LANES]
            xq2 = jnp.concatenate([xq, xq], axis=0)
            xbd = jnp.where(bd_mask, xq2, jnp.zeros_like(xq2))
            yd.append(_dot(m2, xbd))
        y = jnp.concatenate(yd, axis=1) + y_off + dskip_ref[...] * xc[r0:r0 + CHUNK, :]
        ybuf[r0:r0 + CHUNK, :] = y
        s_new = _dot_tn(bm_b, xwt[r0:r0 + CHUNK, :])
        state[...] = st * e_all[r0 + CHUNK - 1:r0 + CHUNK, :] + jnp.where(state_mask, s_new, 0.0)

    y = ybuf[...] * _silu(z_ref[...].astype(F32))
    half = d // SSD_GROUPS
    ng = ng_ref[...]
    for g in range(SSD_GROUPS):
        yg = y[:, g * half:(g + 1) * half]
        o_ref[:, d + g * half:d + (g + 1) * half] = _rms(yg, ng[:, g * half:(g + 1) * half]).astype(o_ref.dtype)


def conv_ssd_mixer(proj, dt_raw, cw, xw, bcw, xb, bcb, dtb, alog, dskip, ng, expand, *, batch, seq, blk):
    t = proj.shape[0]
    d = 1024
    nb = seq // blk
    row = lambda b, s: b * nb + s
    full = lambda shape: pl.BlockSpec(shape, lambda b, s: (0, 0))
    return pl.pallas_call(
        _mixer_kernel,
        grid=(batch, nb),
        in_specs=[
            pl.BlockSpec((blk, d), lambda b, s: (row(b, s), 0)),
            pl.BlockSpec((blk, d), lambda b, s: (row(b, s), 1)),
            pl.BlockSpec((blk, d), lambda b, s: (row(b, s), 2)),
            pl.BlockSpec((blk, d), lambda b, s: (row(b, s), 3)),
            pl.BlockSpec((blk, d), lambda b, s: (row(b, s), 4)),
            pl.BlockSpec((blk, 2 * LANES), lambda b, s: (row(b, s), 5 * d // (2 * LANES))),
            pl.BlockSpec((blk, LANES), lambda b, s: (row(b, s), 0)),
            full(cw.shape), full(xw.shape), full(bcw.shape), full(xb.shape), full(bcb.shape),
            full(dtb.shape), full(alog.shape), full(dskip.shape), full(ng.shape), full(expand.shape),
        ],
        out_specs=pl.BlockSpec((blk, 2 * d), lambda b, s: (row(b, s), 0)),
        out_shape=jax.ShapeDtypeStruct((t, 2 * d), BF16),
        scratch_shapes=[
            pltpu.VMEM((d // LANES, blk + HALO, LANES), F32),
            pltpu.VMEM((d // LANES, blk + HALO, LANES), F32),
            pltpu.VMEM((2, blk + HALO, LANES), F32),
            pltpu.VMEM((SSD_GROUPS * HEAD, d), F32),
            pltpu.VMEM((blk, d), F32),
        ],
        compiler_params=_params("parallel", "arbitrary"),
        name="conv_ssd_mixer",
    )(proj, proj, proj, proj, proj, proj, dt_raw, cw, xw, bcw, xb, bcb, dtb, alog, dskip, ng, expand)


def _proj_res_norm_kernel(a_ref, w_ref, x_ref, g_ref, xo_ref, ho_ref):
    y = x_ref[...] + _dot(a_ref[...], w_ref[...])
    xo_ref[...] = y
    ho_ref[...] = _rms(y, g_ref[...]).astype(ho_ref.dtype)


def proj_res_norm(a, w, x, g, *, tm, h_dtype):
    t, k = a.shape
    d = w.shape[1]
    return pl.pallas_call(
        _proj_res_norm_kernel,
        grid=(t // tm,),
        in_specs=[
            pl.BlockSpec((tm, k), lambda i: (i, 0)),
            pl.BlockSpec((k, d), lambda i: (0, 0)),
            pl.BlockSpec((tm, d), lambda i: (i, 0)),
            pl.BlockSpec((1, d), lambda i: (0, 0)),
        ],
        out_specs=[
            pl.BlockSpec((tm, d), lambda i: (i, 0)),
            pl.BlockSpec((tm, d), lambda i: (i, 0)),
        ],
        out_shape=[
            jax.ShapeDtypeStruct((t, d), F32),
            jax.ShapeDtypeStruct((t, d), h_dtype),
        ],
        compiler_params=_params("parallel"),
        name="proj_res_norm",
    )(a, w, x, g)


SWIGLU_ROWS = 512


def _swiglu_accumulate(x_ref, wg, wu, wd, o_ref):
    rows = min(SWIGLU_ROWS, x_ref.shape[0])
    for r0 in range(0, x_ref.shape[0], rows):
        x = x_ref[r0:r0 + rows, :]
        g = _dot(x, wg)
        u = _dot(x, wu)
        o_ref[r0:r0 + rows, :] += _dot((_silu(g) * u).astype(BF16), wd)


def _ffn_kernel(h_ref, x_ref, wg_ref, wu_ref, wd_ref, o_ref):
    @pl.when(pl.program_id(1) == 0)
    def _():
        o_ref[...] = x_ref[...]

    _swiglu_accumulate(h_ref, wg_ref[...], wu_ref[...], wd_ref[...], o_ref)


def ffn(h, x, wg, wu, wd, *, tm, tf):
    t, d = x.shape
    f = wg.shape[1]
    return pl.pallas_call(
        _ffn_kernel,
        grid=(t // tm, f // tf),
        in_specs=[
            pl.BlockSpec((tm, d), lambda i, j: (i, 0)),
            pl.BlockSpec((tm, d), lambda i, j: (i, 0)),
            pl.BlockSpec((d, tf), lambda i, j: (0, j)),
            pl.BlockSpec((d, tf), lambda i, j: (0, j)),
            pl.BlockSpec((tf, d), lambda i, j: (j, 0)),
        ],
        out_specs=pl.BlockSpec((tm, d), lambda i, j: (i, 0)),
        out_shape=jax.ShapeDtypeStruct((t, d), F32),
        compiler_params=_params("parallel", "arbitrary"),
        name="ffn",
    )(h, x, wg, wu, wd)


def _qkv_kernel(x_ref, g_ref, w_ref, hg_ref, cos_ref, sin_ref, p_ref, o_ref, h_ref):
    j = pl.program_id(1)

    @pl.when(j == 0)
    def _():
        h_ref[...] = _rms(x_ref[...], g_ref[...]).astype(BF16)

    y = _dot(h_ref[...], w_ref[...])

    @pl.when(j < 2)
    def _():
        d = y.shape[1]
        ms = _dot((y * y).astype(BF16), p_ref[...])
        yn = y * lax.rsqrt(ms + EPS) * hg_ref[0]
        lane = lax.broadcasted_iota(I32, y.shape, 1)
        partner = jnp.where((lane % HEAD) < HEAD // 2,
                            pltpu.roll(yn, d - HEAD // 2, 1), pltpu.roll(yn, HEAD // 2, 1))
        reps = d // LANES
        cos = jnp.concatenate([cos_ref[...]] * reps, axis=1)
        sin = jnp.concatenate([sin_ref[...]] * reps, axis=1)
        o_ref[...] = (yn * cos + partner * sin).astype(o_ref.dtype)

    @pl.when(j == 2)
    def _():
        o_ref[...] = y.astype(o_ref.dtype)


def qkv_proj(x, g, w, head_gain, cos_t, sin_t, pavg, *, tm, seq):
    t, d = x.shape
    ns = seq // tm
    return pl.pallas_call(
        _qkv_kernel,
        grid=(t // tm, 3),
        in_specs=[
            pl.BlockSpec((tm, d), lambda i, j: (i, 0)),
            pl.BlockSpec((1, d), lambda i, j: (0, 0)),
            pl.BlockSpec((d, d), lambda i, j: (0, j)),
            pl.BlockSpec((1, 1, d), lambda i, j: (jnp.minimum(j, 1), 0, 0)),
            pl.BlockSpec((tm, LANES), lambda i, j: (i % ns, 0)),
            pl.BlockSpec((tm, LANES), lambda i, j: (i % ns, 0)),
            pl.BlockSpec((d, d), lambda i, j: (0, 0)),
        ],
        out_specs=pl.BlockSpec((tm, d), lambda i, j: (i, j)),
        out_shape=jax.ShapeDtypeStruct((t, 3 * d), BF16),
        scratch_shapes=[pltpu.VMEM((tm, d), BF16)],
        compiler_params=_params("parallel", "arbitrary"),
        name="qkv_proj",
    )(x, g, w, head_gain, cos_t, sin_t, pavg)


def _attn_kernel(q_ref, k_ref, v_ref, lam_ref, sn_ref, o_ref, m_ref, l_ref, acc_ref, s0_ref, s1_ref, *,
                 lambda_init):
    tq = q_ref.shape[0]
    i = pl.program_id(2)
    q = q_ref[...]
    lane = lax.broadcasted_iota(I32, q.shape, 1)
    zero = jnp.zeros_like(q)
    q2 = jnp.concatenate([jnp.where(lane < HEAD, q, zero), jnp.where(lane >= HEAD, q, zero)], axis=0)

    m_ref[...] = jnp.full(m_ref.shape, NEG, F32)
    l_ref[...] = jnp.zeros(l_ref.shape, F32)
    acc_ref[...] = jnp.zeros(acc_ref.shape, F32)

    def scores(kb, dst):
        start = pl.multiple_of(kb * tq, tq)
        dst[...] = _dot_nt(k_ref[pl.ds(start, tq), :], q2)

    def update(kb, src, masked):
        start = pl.multiple_of(kb * tq, tq)
        v = v_ref[pl.ds(start, tq), :]
        s = src[...]
        if masked:
            ki = lax.broadcasted_iota(I32, s.shape, 0)
            qi = lax.broadcasted_iota(I32, s.shape, 1) % tq
            s = jnp.where((ki // CHUNK) <= (qi // CHUNK), s, NEG)
        m_old = m_ref[...]
        m_new = jnp.maximum(m_old, jnp.max(s, axis=0, keepdims=True))
        alpha = jnp.exp2(m_old - m_new)
        p = jnp.exp2(s - m_new)
        l_ref[...] = alpha * l_ref[...] + jnp.sum(p, axis=0, keepdims=True)
        acc_ref[...] = alpha * acc_ref[...] + _dot_tn(v, p.astype(BF16))
        m_ref[...] = m_new

    scores(0, s0_ref)

    def body(jj, c):
        kb = 2 * jj
        scores(kb + 1, s1_ref)
        update(kb, s0_ref, False)
        scores(kb + 2, s0_ref)
        update(kb + 1, s1_ref, False)
        return c

    lax.fori_loop(0, i // 2, body, 0)

    @pl.when(i % 2 == 1)
    def _():
        scores(i, s1_ref)
        update(i - 1, s0_ref, False)
        update(i, s1_ref, True)

    @pl.when(i % 2 == 0)
    def _():
        update(i, s0_ref, True)

    lam_p = lam_ref[...]
    lam = (jnp.exp(jnp.sum(lam_p[0:1] * lam_p[1:2], axis=-1, keepdims=True))
           - jnp.exp(jnp.sum(lam_p[2:3] * lam_p[3:4], axis=-1, keepdims=True)) + lambda_init)
    o = acc_ref[...] / l_ref[...]
    o = (o[:, 0:tq] - lam * o[:, tq:2 * tq]).T
    o_ref[...] = (_rms(o, sn_ref[...]) * (1.0 - lambda_init)).astype(o_ref.dtype)


def diff_attention(qkv, lam_p, sub_norm, *, batch, seq, heads, tq, lambda_init):
    t = qkv.shape[0]
    nq = seq // tq
    return pl.pallas_call(
        functools.partial(_attn_kernel, lambda_init=lambda_init),
        grid=(batch, heads, nq),
        in_specs=[
            pl.BlockSpec((tq, LANES), lambda b, h, i: (b * nq + i, h)),
            pl.BlockSpec((seq, LANES), lambda b, h, i: (b, heads + h)),
            pl.BlockSpec((seq, LANES), lambda b, h, i: (b, 2 * heads + h)),
            pl.BlockSpec(lam_p.shape, lambda b, h, i: (0, 0)),
            pl.BlockSpec(sub_norm.shape, lambda b, h, i: (0, 0)),
        ],
        out_specs=pl.BlockSpec((tq, LANES), lambda b, h, i: (b * nq + i, h)),
        out_shape=jax.ShapeDtypeStruct((t, heads * LANES), BF16),
        scratch_shapes=[
            pltpu.VMEM((1, 2 * tq), F32),
            pltpu.VMEM((1, 2 * tq), F32),
            pltpu.VMEM((LANES, 2 * tq), F32),
            pltpu.VMEM((tq, 2 * tq), F32),
            pltpu.VMEM((tq, 2 * tq), F32),
        ],
        compiler_params=_params("parallel", "parallel", "arbitrary"),
        name="diff_attention",
    )(qkv, qkv, qkv, lam_p, sub_norm)


def _router_kernel(h_ref, w_ref, meta_ref, cnt_ref, run_ref):
    tm = h_ref.shape[0]

    @pl.when(pl.program_id(0) == 0)
    def _():
        run_ref[...] = jnp.zeros(run_ref.shape, F32)

    h = h_ref[...]
    w = w_ref[...]
    h_hi = h.astype(BF16)
    h_lo = (h - h_hi.astype(F32)).astype(BF16)
    w_hi = w.astype(BF16)
    w_lo = (w - w_hi.astype(F32)).astype(BF16)
    logits = _dot(h_hi, w_hi) + (_dot(h_hi, w_lo) + _dot(h_lo, w_hi))
    lane = lax.broadcasted_iota(I32, logits.shape, 1)
    lane_f = lane.astype(F32)
    logits = jnp.where(lane < N_EXPERTS, logits, NEG)
    m1 = jnp.max(logits, axis=-1, keepdims=True)
    i1 = jnp.min(jnp.where(logits == m1, lane_f, float(LANES)), axis=-1, keepdims=True)
    rest = jnp.where(lane_f == i1, NEG, logits)
    m2 = jnp.max(rest, axis=-1, keepdims=True)
    i2 = jnp.min(jnp.where(rest == m2, lane_f, float(LANES)), axis=-1, keepdims=True)
    e2 = jnp.exp(m2 - m1)
    g1 = 1.0 / (1.0 + e2)
    g2 = e2 * g1

    oh1 = lane_f == i1
    oh2 = lane_f == i2
    both = jnp.where(oh1, 1.0, 0.0) + jnp.where(oh2, 1.0, 0.0)
    ri = lax.broadcasted_iota(I32, (tm, tm), 0)
    ci = lax.broadcasted_iota(I32, (tm, tm), 1)
    strict = jnp.where(ri > ci, 1.0, 0.0).astype(BF16)
    before = run_ref[...] + _dot(strict, both.astype(BF16))
    r1 = jnp.sum(jnp.where(oh1, before, 0.0), axis=-1, keepdims=True)
    r2 = jnp.sum(jnp.where(oh2, before, 0.0), axis=-1, keepdims=True)
    run_ref[...] += jnp.sum(both, axis=0, keepdims=True)
    cnt_ref[...] = run_ref[...]

    meta = jnp.where(lane == 0, g1, jnp.where(lane == 1, g2, 0.0))
    meta = jnp.where(lane == 2, r1, jnp.where(lane == 3, r2, meta))
    meta_ref[...] = jnp.where(lane == 4, i1, jnp.where(lane == 5, i2, meta))


def router(h, w_pad, *, tm):
    t, d = h.shape
    return pl.pallas_call(
        _router_kernel,
        grid=(t // tm,),
        in_specs=[
            pl.BlockSpec((tm, d), lambda i: (i, 0)),
            pl.BlockSpec((d, LANES), lambda i: (0, 0)),
        ],
        out_specs=[
            pl.BlockSpec((tm, LANES), lambda i: (i, 0)),
            pl.BlockSpec((1, LANES), lambda i: (0, 0)),
        ],
        out_shape=[
            jax.ShapeDtypeStruct((t, LANES), F32),
            jax.ShapeDtypeStruct((1, LANES), F32),
        ],
        scratch_shapes=[pltpu.VMEM((1, LANES), F32)],
        compiler_params=_params("arbitrary"),
        name="moe_router",
    )(h, w_pad)


def _dispatch_kernel(pad_ref, dest_hbm, h_ref, xb_hbm, idx_smem, zrow, isem, sem, *, td):
    i = pl.program_id(0)
    cp = pltpu.make_async_copy(dest_hbm.at[i], idx_smem, isem)
    cp.start()
    cp.wait()

    def body(r, c):
        for k in range(2):
            pltpu.make_async_copy(h_ref.at[pl.ds(r, 1)], xb_hbm.at[pl.ds(idx_smem[2 * r + k], 1)], sem).start()
        return c

    lax.fori_loop(0, td, body, 0, unroll=8)

    @pl.when(i == 0)
    def _():
        zrow[...] = jnp.zeros(zrow.shape, zrow.dtype)
        nrange = pad_ref.shape[0] // 2
        for e in range(nrange):
            start = pad_ref[e]
            n = pad_ref[nrange + e]

            def zbody(r, c):
                pltpu.make_async_copy(zrow, xb_hbm.at[pl.ds(start + r, 1)], sem).start()
                return c

            lax.fori_loop(0, n, zbody, 0)

            def wbody(r, c):
                pltpu.make_async_copy(zrow, xb_hbm.at[pl.ds(0, 1)], sem).wait()
                return c

            lax.fori_loop(0, n, wbody, 0)

    for k in range(2):
        pltpu.make_async_copy(h_ref, xb_hbm.at[pl.ds(0, td)], sem).wait()


def moe_dispatch(h, dest, pad_info, *, cap, td):
    t, d = h.shape
    return pl.pallas_call(
        functools.partial(_dispatch_kernel, td=td),
        grid_spec=pltpu.PrefetchScalarGridSpec(
            num_scalar_prefetch=1,
            grid=(t // td,),
            in_specs=[
                pl.BlockSpec(memory_space=pl.ANY),
                pl.BlockSpec((td, d), lambda i, pad: (i, 0)),
            ],
            out_specs=pl.BlockSpec(memory_space=pl.ANY),
            scratch_shapes=[
                pltpu.SMEM((2 * td,), I32),
                pltpu.VMEM((1, d), h.dtype),
                pltpu.SemaphoreType.DMA,
                pltpu.SemaphoreType.DMA,
            ],
        ),
        out_shape=jax.ShapeDtypeStruct((cap, d), h.dtype),
        compiler_params=_params("arbitrary"),
        name="moe_dispatch",
    )(pad_info, dest, h)


def _experts_kernel(be_ref, x_ref, wg_ref, wu_ref, wd_ref, o_ref, xb_ref):
    i = pl.program_id(0)
    j = pl.program_id(1)
    used = i < be_ref[be_ref.shape[0] - 1]

    @pl.when(j == 0)
    def _():
        o_ref[...] = jnp.zeros(o_ref.shape, o_ref.dtype)

    @pl.when(used & (j == 0))
    def _():
        xb_ref[...] = x_ref[...].astype(BF16)

    @pl.when(used)
    def _():
        _swiglu_accumulate(xb_ref, wg_ref[0], wu_ref[0], wd_ref[0], o_ref)


def moe_experts(block_e, xb, wg, wu, wd, *, bm, tf):
    cap, d = xb.shape
    f = wg.shape[2]
    nblk = cap // bm
    return pl.pallas_call(
        _experts_kernel,
        grid_spec=pltpu.PrefetchScalarGridSpec(
            num_scalar_prefetch=1,
            grid=(nblk, f // tf),
            in_specs=[
                pl.BlockSpec((bm, d), lambda i, j, be: (jnp.minimum(i, be[nblk] - 1), 0)),
                pl.BlockSpec((1, d, tf), lambda i, j, be: (be[i], 0, j)),
                pl.BlockSpec((1, d, tf), lambda i, j, be: (be[i], 0, j)),
                pl.BlockSpec((1, tf, d), lambda i, j, be: (be[i], j, 0)),
            ],
            out_specs=pl.BlockSpec((bm, d), lambda i, j, be: (i, 0)),
            scratch_shapes=[pltpu.VMEM((bm, d), BF16)],
        ),
        out_shape=jax.ShapeDtypeStruct((cap, d), F32),
        compiler_params=_params("parallel", "arbitrary"),
        name="moe_experts",
    )(block_e, xb, wg, wu, wd)


def _combine_kernel(dest_hbm, yb_hbm, x_ref, gate_ref, o_ref, idx_smem, y0, y1, isem, sem, *, tc):
    i = pl.program_id(0)
    cp = pltpu.make_async_copy(dest_hbm.at[i], idx_smem, isem)
    cp.start()
    cp.wait()

    def body(r, c):
        pltpu.make_async_copy(yb_hbm.at[pl.ds(idx_smem[2 * r], 1)], y0.at[pl.ds(r, 1)], sem).start()
        pltpu.make_async_copy(yb_hbm.at[pl.ds(idx_smem[2 * r + 1], 1)], y1.at[pl.ds(r, 1)], sem).start()
        return c

    lax.fori_loop(0, tc, body, 0, unroll=8)
    pltpu.make_async_copy(yb_hbm.at[pl.ds(0, tc)], y0, sem).wait()
    pltpu.make_async_copy(yb_hbm.at[pl.ds(0, tc)], y1, sem).wait()
    gate = gate_ref[...]
    o_ref[...] = x_ref[...] + (gate[:, 0:1] * y0[...] + gate[:, 1:2] * y1[...])


def moe_combine(dest, yb, x, gates, *, tc):
    t, d = x.shape
    return pl.pallas_call(
        functools.partial(_combine_kernel, tc=tc),
        grid=(t // tc,),
        in_specs=[
            pl.BlockSpec(memory_space=pl.ANY),
            pl.BlockSpec(memory_space=pl.ANY),
            pl.BlockSpec((tc, d), lambda i: (i, 0)),
            pl.BlockSpec((tc, LANES), lambda i: (i, 0)),
        ],
        out_specs=pl.BlockSpec((tc, d), lambda i: (i, 0)),
        out_shape=jax.ShapeDtypeStruct((t, d), F32),
        scratch_shapes=[
            pltpu.SMEM((2 * tc,), I32),
            pltpu.VMEM((tc, d), F32),
            pltpu.VMEM((tc, d), F32),
            pltpu.SemaphoreType.DMA,
            pltpu.SemaphoreType.DMA,
        ],
        compiler_params=_params("arbitrary"),
        name="moe_combine",
    )(dest, yb, x, gates)


def _tile(n, pref):
    t = min(n, pref)
    while n % t:
        t //= 2
    return t


def even_layer(x, p, *, batch, seq):
    t, d = x.shape
    tm = _tile(t, 1024)
    proj, dt_raw = norm_proj(x, p["norm_mix"], p["w_in_main"], p["w_in_dt"], tm=tm, tn=p["tn_in"])
    mix = conv_ssd_mixer(proj, dt_raw, p["conv_w"], p["xw"], p["bcw"], p["xb"], p["bcb"], p["dt_bias"],
                         p["a_log"], p["d_skip"], p["ssd_norm_g"], p["expand"],
                         batch=batch, seq=seq, blk=_tile(seq, 256))
    x, h = proj_res_norm(mix, p["w_out"], x, p["norm_ffn"], tm=_tile(t, 512), h_dtype=BF16)
    return ffn(h, x, p["w_gate"], p["w_up"], p["w_down"], tm=tm, tf=FF_TILE)


def moe_layer(x, h, p, *, bm):
    t, d = x.shape
    tr = _tile(t, 512)
    meta, counts = router(h, p["w_router"], tm=tr)
    rank = meta[:, 2:4].astype(I32)
    top_e = meta[:, 4:6].astype(I32)
    counts = counts[0, :N_EXPERTS].astype(I32)
    padded = (counts + bm - 1) // bm * bm
    pends = jnp.cumsum(padded)
    pstarts = pends - padded
    dest = pstarts[top_e] + rank
    nblk = (2 * t) // bm + N_EXPERTS
    cap = nblk * bm
    block_e = jnp.minimum(jnp.searchsorted(pends, jnp.arange(nblk, dtype=I32) * bm, side="right"),
                          N_EXPERTS - 1).astype(I32)
    block_e = jnp.concatenate([block_e, (pends[-1:] // bm).astype(I32)])
    pad_info = jnp.concatenate([pstarts + counts, pends[-1:], padded - counts, cap - pends[-1:]]).astype(I32)
    td = _tile(t, 512)
    dest_t = dest.reshape(t // td, 2 * td)
    xb = moe_dispatch(h, dest_t, pad_info, cap=cap, td=td)
    yb = moe_experts(block_e, xb, p["w_gate"], p["w_up"], p["w_down"], bm=bm, tf=FF_TILE)
    return moe_combine(dest_t, yb, x, meta, tc=td)


def odd_layer(x, p, *, batch, seq, lambda_init, cos_t, sin_t):
    t, d = x.shape
    tm = _tile(seq, 1024)
    qkv = qkv_proj(x, p["norm_mix"], p["w_qkv"], p["head_gain"], cos_t, sin_t, p["pavg"], tm=tm, seq=seq)
    o = diff_attention(qkv, p["lam"], p["sub_norm"], batch=batch, seq=seq, heads=d // LANES,
                       tq=_tile(seq, 512), lambda_init=lambda_init)
    x, h = proj_res_norm(o, p["w_o"], x, p["norm_ffn"], tm=_tile(t, 512), h_dtype=F32)
    return moe_layer(x, h, p, bm=_tile(t, 1024))


def _rope_tables(seq):
    inv = 1.0 / (ROPE_THETA ** (jnp.arange(0, HEAD, 2, dtype=F32) / HEAD))
    ang = jnp.arange(seq, dtype=F32)[:, None] * inv[None, :]
    cos, sin = jnp.cos(ang), jnp.sin(ang)
    cos_t = jnp.concatenate([cos, cos, cos, cos], axis=1)
    sin_t = jnp.concatenate([-sin, sin, -sin, sin], axis=1)
    return cos_t, sin_t


def kernel(x, even_norm_mix, even_w_in, even_conv_w, ssd_conv_w, ssd_conv_b, ssd_dt_bias, ssd_a_log, ssd_d, ssd_norm_g, even_w_out, even_norm_ffn, ffn_w_gate, ffn_w_up, ffn_w_down, odd_norm_mix, attn_w_qkv, attn_q_norm, attn_k_norm, attn_lambda_q1, attn_lambda_k1, attn_lambda_q2, attn_lambda_k2, attn_sub_norm, attn_w_o, odd_norm_ffn, moe_w_router, moe_w_gate, moe_w_up, moe_w_down):
    batch, seq, d = x.shape
    depth = even_norm_mix.shape[0] + odd_norm_mix.shape[0]
    n_main = 5 * d + 2 * LANES
    heads = jnp.arange(d) // HEAD
    expand = (jnp.arange(LANES)[:, None] == heads[None, :]).astype(BF16)
    pavg = jnp.where(heads[:, None] == heads[None, :], 1.0 / HEAD, 0.0).astype(BF16)
    cos_t, sin_t = _rope_tables(seq)
    scale = math.log2(math.e) / math.sqrt(HEAD)

    def pad_lanes(v):
        return jnp.pad(v.astype(F32), (0, LANES - v.shape[0]))[None, :]

    xf = x.reshape(batch * seq, d)
    for layer in range(depth):
        i = layer // 2
        if layer % 2 == 0:
            w_in = even_w_in[i]
            p = dict(
                norm_mix=even_norm_mix[i][None, :],
                w_in_main=w_in[:, :n_main].astype(BF16),
                w_in_dt=jnp.pad(w_in[:, n_main:], ((0, 0), (0, LANES - (w_in.shape[1] - n_main)))).astype(BF16),
                tn_in=n_main // 3,
                conv_w=even_conv_w[i],
                xw=ssd_conv_w[i][:, :d], bcw=ssd_conv_w[i][:, d:],
                xb=ssd_conv_b[i][None, :d], bcb=ssd_conv_b[i][None, d:],
                dt_bias=pad_lanes(ssd_dt_bias[i]), a_log=pad_lanes(ssd_a_log[i]),
                d_skip=jnp.repeat(ssd_d[i].astype(F32), HEAD)[None, :],
                ssd_norm_g=ssd_norm_g[i][None, :], expand=expand,
                w_out=even_w_out[i].astype(BF16), norm_ffn=even_norm_ffn[i][None, :],
                w_gate=ffn_w_gate[i].astype(BF16), w_up=ffn_w_up[i].astype(BF16),
                w_down=ffn_w_down[i].astype(BF16),
            )
            xf = even_layer(xf, p, batch=batch, seq=seq)
        else:
            lambda_init = 0.8 - 0.6 * math.exp(-0.3 * layer)
            reps = d // HEAD
            head_gain = jnp.stack([jnp.tile(attn_q_norm[i], reps) * scale, jnp.tile(attn_k_norm[i], reps)])[:, None, :]
            lam = jnp.stack([pad_lanes(v)[0] for v in
                             (attn_lambda_q1[i], attn_lambda_k1[i], attn_lambda_q2[i], attn_lambda_k2[i])])
            p = dict(
                norm_mix=odd_norm_mix[i][None, :], w_qkv=attn_w_qkv[i].astype(BF16),
                head_gain=head_gain.astype(F32), pavg=pavg, lam=lam,
                sub_norm=attn_sub_norm[i][None, :], w_o=attn_w_o[i].astype(BF16),
                norm_ffn=odd_norm_ffn[i][None, :],
                w_router=jnp.pad(moe_w_router[i], ((0, 0), (0, LANES - N_EXPERTS))),
                w_gate=moe_w_gate[i].astype(BF16), w_up=moe_w_up[i].astype(BF16),
                w_down=moe_w_down[i].astype(BF16),
            )
            xf = odd_layer(xf, p, batch=batch, seq=seq, lambda_init=lambda_init, cos_t=cos_t, sin_t=sin_t)
    return xf.reshape(batch, seq, d)
```

```python
import functools
import math

import jax
import jax.numpy as jnp
from jax import lax
from jax.experimental import pallas as pl
from jax.experimental.pallas import tpu as pltpu

F32 = jnp.float32
BF16 = jnp.bfloat16
I32 = jnp.int32

EPS = 1e-6
ROPE_THETA = 10000.0
CHUNK = 64
HEAD = 64
LANES = 128
SSD_HEADS = 16
SSD_GROUPS = 2
N_EXPERTS = 8
FF_TILE = 1408
VMEM_LIMIT = 56 * 1024 * 1024
NEG = -1e30


def _params(*sem):
    return pltpu.CompilerParams(dimension_semantics=sem, vmem_limit_bytes=VMEM_LIMIT)


def _rms(x, g):
    ms = jnp.mean(x * x, axis=-1, keepdims=True)
    return x * lax.rsqrt(ms + EPS) * g


def _silu(x):
    return x * (1.0 / (1.0 + jnp.exp(-x)))


def _dot(a, b):
    return jnp.dot(a, b, preferred_element_type=F32)


def _dot_nt(a, b):
    return lax.dot_general(a, b, (((1,), (1,)), ((), ())), preferred_element_type=F32)


def _dot_tn(a, b):
    return lax.dot_general(a, b, (((0,), (0,)), ((), ())), preferred_element_type=F32)


def _norm_proj_kernel(x_ref, g_ref, w_ref, wdt_ref, o_ref, odt_ref, h_ref):
    @pl.when(pl.program_id(1) == 0)
    def _():
        h = _rms(x_ref[...], g_ref[...]).astype(BF16)
        h_ref[...] = h
        odt_ref[...] = _dot(h, wdt_ref[...])

    o_ref[...] = _dot(h_ref[...], w_ref[...]).astype(o_ref.dtype)


def norm_proj(x, g, w_main, w_dt, *, tm, tn):
    t, d = x.shape
    n = w_main.shape[1]
    return pl.pallas_call(
        _norm_proj_kernel,
        grid=(t // tm, n // tn),
        in_specs=[
            pl.BlockSpec((tm, d), lambda i, j: (i, 0)),
            pl.BlockSpec((1, d), lambda i, j: (0, 0)),
            pl.BlockSpec((d, tn), lambda i, j: (0, j)),
            pl.BlockSpec((d, LANES), lambda i, j: (0, 0)),
        ],
        out_specs=[
            pl.BlockSpec((tm, tn), lambda i, j: (i, j)),
            pl.BlockSpec((tm, LANES), lambda i, j: (i, 0)),
        ],
        out_shape=[
            jax.ShapeDtypeStruct((t, n), BF16),
            jax.ShapeDtypeStruct((t, LANES), F32),
        ],
        scratch_shapes=[pltpu.VMEM((tm, d), BF16)],
        compiler_params=_params("parallel", "arbitrary"),
        name="norm_proj",
    )(x, g, w_main, w_dt)


HALO = 8


def _split3(v):
    hi = v.astype(BF16)
    r = v - hi.astype(F32)
    mid = r.astype(BF16)
    lo = (r - mid.astype(F32)).astype(BF16)
    return hi, mid, lo


def _expand(v, e, terms):
    parts = _split3(v)[:terms]
    out = _dot(parts[0], e)
    for p in parts[1:]:
        out = out + _dot(p, e)
    return out


def _mixer_kernel(bg_ref, cg_ref, hv_ref, z_ref, xs_ref, bc_ref, dtr_ref,
                  cw_ref, xw_ref, bcw_ref, xb_ref, bcb_ref, dtb_ref, alog_ref,
                  dskip_ref, ng_ref, e_ref,
                  o_ref, ubuf, xbuf, bcbuf, state, ybuf):
    L = o_ref.shape[0]
    d = bg_ref.shape[1]
    nchunk = L // CHUNK

    @pl.when(pl.program_id(1) == 0)
    def _():
        for buf in (ubuf, xbuf, bcbuf):
            buf[:, 0:HALO, :] = jnp.zeros((buf.shape[0], HALO, LANES), F32)
        state[...] = jnp.zeros(state.shape, F32)

    def causal_conv(v, buf, w):
        taps = w.shape[0]
        cols = []
        for c in range(v.shape[1] // LANES):
            lanes = slice(c * LANES, (c + 1) * LANES)
            vc = v[:, lanes]
            buf[c, HALO:HALO + L, :] = vc
            acc = w[taps - 1:taps, lanes] * vc
            for k in range(taps - 1):
                off = HALO - (taps - 1) + k
                acc = acc + w[k:k + 1, lanes] * buf[c, off:off + L, :]
            buf[c, 0:HALO, :] = buf[c, L:L + HALO, :]
            cols.append(acc)
        return jnp.concatenate(cols, axis=1)

    u = cg_ref[...].astype(F32) * hv_ref[...].astype(F32)
    conv = causal_conv(u, ubuf, cw_ref[...])
    o_ref[:, 0:d] = (bg_ref[...].astype(F32) * conv).astype(o_ref.dtype)

    xc = _silu(causal_conv(xs_ref[...].astype(F32), xbuf, xw_ref[...]) + xb_ref[...])
    bcc = _silu(causal_conv(bc_ref[...].astype(F32), bcbuf, bcw_ref[...]) + bcb_ref[...])

    e = e_ref[...]
    dtv = dtr_ref[...] + dtb_ref[...]
    dt = jnp.maximum(dtv, 0.0) + jnp.log(1.0 + jnp.exp(-jnp.abs(dtv)))
    a = -jnp.exp(alog_ref[...])
    da = dt * a

    ri = lax.broadcasted_iota(I32, (L, L), 0)
    ci = lax.broadcasted_iota(I32, (L, L), 1)
    tri = jnp.where((ri >= ci) & ((ri // CHUNK) == (ci // CHUNK)), 1.0, 0.0).astype(BF16)
    d_hi, d_mid, d_lo = _split3(da)
    acs = _dot(tri, d_hi) + _dot(tri, d_mid) + _dot(tri, d_lo)

    last = jnp.concatenate(
        [jnp.broadcast_to(acs[c * CHUNK + CHUNK - 1:c * CHUNK + CHUNK, :], (CHUNK, LANES))
         for c in range(nchunk)], axis=0)
    col_all = _expand(acs, e, 3)
    e_all = jnp.exp(col_all)
    ds_all = jnp.exp(_expand(last, e, 3) - col_all)
    xdt = xc * _expand(dt, e, 3)
    xwt = (xdt * ds_all).astype(BF16)
    xdt_b = xdt.astype(BF16)

    lane = lax.broadcasted_iota(I32, (CHUNK, LANES), 1)
    row = lax.broadcasted_iota(I32, (CHUNK, LANES), 0)
    low_half = lane < HEAD
    tri2 = row >= (lane % HEAD)
    lane2 = lax.broadcasted_iota(I32, (2 * CHUNK, LANES), 1)
    row2 = lax.broadcasted_iota(I32, (2 * CHUNK, LANES), 0)
    bd_mask = (lane2 < HEAD) == (row2 < CHUNK)
    srow = lax.broadcasted_iota(I32, state.shape, 0)
    scol = lax.broadcasted_iota(I32, state.shape, 1)
    state_mask = (srow // HEAD) == (scol // (d // SSD_GROUPS))

    for c in range(nchunk):
        r0 = c * CHUNK
        bm = bcc[r0:r0 + CHUNK, 0:LANES]
        cm = bcc[r0:r0 + CHUNK, LANES:2 * LANES]
        bm_b = bm.astype(BF16)
        cm_b = cm.astype(BF16)
        acs_c = acs[r0:r0 + CHUNK, :]
        acs_t = jnp.concatenate([acs_c, acs_c], axis=0).T
        st = state[...]
        y_off = _dot(cm_b, st.astype(BF16)) * e_all[r0:r0 + CHUNK, :]
        bm2 = jnp.concatenate([bm_b, bm_b], axis=0)
        cb2 = []
        for g in range(SSD_GROUPS):
            cmm = jnp.where((lane // HEAD) == g, cm, 0.0).astype(BF16)
            cb2.append(_dot_nt(cmm, bm2))
        yd = []
        for q in range(SSD_HEADS // 2):
            colq = col_all[r0:r0 + CHUNK, q * LANES:(q + 1) * LANES]
            rowq = jnp.where(low_half[0:1], acs_t[2 * q:2 * q + 1, :], acs_t[2 * q + 1:2 * q + 2, :])
            dec = jnp.exp(jnp.where(tri2, colq - rowq, NEG))
            m2 = (cb2[q // (SSD_HEADS // 2 // SSD_GROUPS)] * dec).astype(BF16)
            xq = xdt_b[r0:r0 + CHUNK, q * LANES:(q + 1) * LANES]
            xq2 = jnp.concatenate([xq, xq], axis=0)
            xbd = jnp.where(bd_mask, xq2, jnp.zeros_like(xq2))
            yd.append(_dot(m2, xbd))
        y = jnp.concatenate(yd, axis=1) + y_off + dskip_ref[...] * xc[r0:r0 + CHUNK, :]
        ybuf[r0:r0 + CHUNK, :] = y
        s_new = _dot_tn(bm_b, xwt[r0:r0 + CHUNK, :])
        state[...] = st * e_all[r0 + CHUNK - 1:r0 + CHUNK, :] + jnp.where(state_mask, s_new, 0.0)

    y = ybuf[...] * _silu(z_ref[...].astype(F32))
    half = d // SSD_GROUPS
    ng = ng_ref[...]
    for g in range(SSD_GROUPS):
        yg = y[:, g * half:(g + 1) * half]
        o_ref[:, d + g * half:d + (g + 1) * half] = _rms(yg, ng[:, g * half:(g + 1) * half]).astype(o_ref.dtype)


def conv_ssd_mixer(proj, dt_raw, cw, xw, bcw, xb, bcb, dtb, alog, dskip, ng, expand, *, batch, seq, blk):
    t = proj.shape[0]
    d = 1024
    nb = seq // blk
    row = lambda b, s: b * nb + s
    full = lambda shape: pl.BlockSpec(shape, lambda b, s: (0, 0))
    return pl.pallas_call(
        _mixer_kernel,
        grid=(batch, nb),
        in_specs=[
            pl.BlockSpec((blk, d), lambda b, s: (row(b, s), 0)),
            pl.BlockSpec((blk, d), lambda b, s: (row(b, s), 1)),
            pl.BlockSpec((blk, d), lambda b, s: (row(b, s), 2)),
            pl.BlockSpec((blk, d), lambda b, s: (row(b, s), 3)),
            pl.BlockSpec((blk, d), lambda b, s: (row(b, s), 4)),
            pl.BlockSpec((blk, 2 * LANES), lambda b, s: (row(b, s), 5 * d // (2 * LANES))),
            pl.BlockSpec((blk, LANES), lambda b, s: (row(b, s), 0)),
            full(cw.shape), full(xw.shape), full(bcw.shape), full(xb.shape), full(bcb.shape),
            full(dtb.shape), full(alog.shape), full(dskip.shape), full(ng.shape), full(expand.shape),
        ],
        out_specs=pl.BlockSpec((blk, 2 * d), lambda b, s: (row(b, s), 0)),
        out_shape=jax.ShapeDtypeStruct((t, 2 * d), BF16),
        scratch_shapes=[
            pltpu.VMEM((d // LANES, blk + HALO, LANES), F32),
            pltpu.VMEM((d // LANES, blk + HALO, LANES), F32),
            pltpu.VMEM((2, blk + HALO, LANES), F32),
            pltpu.VMEM((SSD_GROUPS * HEAD, d), F32),
            pltpu.VMEM((blk, d), F32),
        ],
        compiler_params=_params("parallel", "arbitrary"),
        name="conv_ssd_mixer",
    )(proj, proj, proj, proj, proj, proj, dt_raw, cw, xw, bcw, xb, bcb, dtb, alog, dskip, ng, expand)


def _proj_res_norm_kernel(a_ref, w_ref, x_ref, g_ref, xo_ref, ho_ref):
    y = x_ref[...] + _dot(a_ref[...], w_ref[...])
    xo_ref[...] = y
    ho_ref[...] = _rms(y, g_ref[...]).astype(ho_ref.dtype)


def proj_res_norm(a, w, x, g, *, tm, h_dtype):
    t, k = a.shape
    d = w.shape[1]
    return pl.pallas_call(
        _proj_res_norm_kernel,
        grid=(t // tm,),
        in_specs=[
            pl.BlockSpec((tm, k), lambda i: (i, 0)),
            pl.BlockSpec((k, d), lambda i: (0, 0)),
            pl.BlockSpec((tm, d), lambda i: (i, 0)),
            pl.BlockSpec((1, d), lambda i: (0, 0)),
        ],
        out_specs=[
            pl.BlockSpec((tm, d), lambda i: (i, 0)),
            pl.BlockSpec((tm, d), lambda i: (i, 0)),
        ],
        out_shape=[
            jax.ShapeDtypeStruct((t, d), F32),
            jax.ShapeDtypeStruct((t, d), h_dtype),
        ],
        compiler_params=_params("parallel"),
        name="proj_res_norm",
    )(a, w, x, g)


SWIGLU_ROWS = 512


def _swiglu_accumulate(x_ref, wg, wu, wd, o_ref):
    rows = min(SWIGLU_ROWS, x_ref.shape[0])
    for r0 in range(0, x_ref.shape[0], rows):
        x = x_ref[r0:r0 + rows, :]
        g = _dot(x, wg)
        u = _dot(x, wu)
        o_ref[r0:r0 + rows, :] += _dot((_silu(g) * u).astype(BF16), wd)


def _ffn_kernel(h_ref, x_ref, wg_ref, wu_ref, wd_ref, o_ref):
    @pl.when(pl.program_id(1) == 0)
    def _():
        o_ref[...] = x_ref[...]

    _swiglu_accumulate(h_ref, wg_ref[...], wu_ref[...], wd_ref[...], o_ref)


def ffn(h, x, wg, wu, wd, *, tm, tf):
    t, d = x.shape
    f = wg.shape[1]
    return pl.pallas_call(
        _ffn_kernel,
        grid=(t // tm, f // tf),
        in_specs=[
            pl.BlockSpec((tm, d), lambda i, j: (i, 0)),
            pl.BlockSpec((tm, d), lambda i, j: (i, 0)),
            pl.BlockSpec((d, tf), lambda i, j: (0, j)),
            pl.BlockSpec((d, tf), lambda i, j: (0, j)),
            pl.BlockSpec((tf, d), lambda i, j: (j, 0)),
        ],
        out_specs=pl.BlockSpec((tm, d), lambda i, j: (i, 0)),
        out_shape=jax.ShapeDtypeStruct((t, d), F32),
        compiler_params=_params("parallel", "arbitrary"),
        name="ffn",
    )(h, x, wg, wu, wd)


def _qkv_kernel(x_ref, g_ref, w_ref, hg_ref, cos_ref, sin_ref, p_ref, o_ref, h_ref):
    j = pl.program_id(1)

    @pl.when(j == 0)
    def _():
        h_ref[...] = _rms(x_ref[...], g_ref[...]).astype(BF16)

    y = _dot(h_ref[...], w_ref[...])

    @pl.when(j < 2)
    def _():
        d = y.shape[1]
        ms = _dot((y * y).astype(BF16), p_ref[...])
        yn = y * lax.rsqrt(ms + EPS) * hg_ref[0]
        lane = lax.broadcasted_iota(I32, y.shape, 1)
        partner = jnp.where((lane % HEAD) < HEAD // 2,
                            pltpu.roll(yn, d - HEAD // 2, 1), pltpu.roll(yn, HEAD // 2, 1))
        reps = d // LANES
        cos = jnp.concatenate([cos_ref[...]] * reps, axis=1)
        sin = jnp.concatenate([sin_ref[...]] * reps, axis=1)
        o_ref[...] = (yn * cos + partner * sin).astype(o_ref.dtype)

    @pl.when(j == 2)
    def _():
        o_ref[...] = y.astype(o_ref.dtype)


def qkv_proj(x, g, w, head_gain, cos_t, sin_t, pavg, *, tm, seq):
    t, d = x.shape
    ns = seq // tm
    return pl.pallas_call(
        _qkv_kernel,
        grid=(t // tm, 3),
        in_specs=[
            pl.BlockSpec((tm, d), lambda i, j: (i, 0)),
            pl.BlockSpec((1, d), lambda i, j: (0, 0)),
            pl.BlockSpec((d, d), lambda i, j: (0, j)),
            pl.BlockSpec((1, 1, d), lambda i, j: (jnp.minimum(j, 1), 0, 0)),
            pl.BlockSpec((tm, LANES), lambda i, j: (i % ns, 0)),
            pl.BlockSpec((tm, LANES), lambda i, j: (i % ns, 0)),
            pl.BlockSpec((d, d), lambda i, j: (0, 0)),
        ],
        out_specs=pl.BlockSpec((tm, d), lambda i, j: (i, j)),
        out_shape=jax.ShapeDtypeStruct((t, 3 * d), BF16),
        scratch_shapes=[pltpu.VMEM((tm, d), BF16)],
        compiler_params=_params("parallel", "arbitrary"),
        name="qkv_proj",
    )(x, g, w, head_gain, cos_t, sin_t, pavg)


def _attn_kernel(q_ref, k_ref, v_ref, lam_ref, sn_ref, o_ref, m_ref, l_ref, acc_ref, s0_ref, s1_ref, *,
                 tq, lambda_init):
    seq = q_ref.shape[0]
    nq = seq // tq
    lane = lax.broadcasted_iota(I32, (tq, LANES), 1)
    lam_p = lam_ref[...]
    lam = (jnp.exp(jnp.sum(lam_p[0:1] * lam_p[1:2], axis=-1, keepdims=True))
           - jnp.exp(jnp.sum(lam_p[2:3] * lam_p[3:4], axis=-1, keepdims=True)) + lambda_init)

    def stacked_q(qi):
        q = q_ref[qi * tq:(qi + 1) * tq, :]
        zero = jnp.zeros_like(q)
        return jnp.concatenate([jnp.where(lane < HEAD, q, zero), jnp.where(lane >= HEAD, q, zero)], axis=0)

    def scores(step, dst):
        qi, kb = step
        dst[...] = _dot_nt(k_ref[kb * tq:(kb + 1) * tq, :], stacked_q(qi))

    def update(step, src):
        qi, kb = step
        s = src[...]
        if kb == qi:
            ki = lax.broadcasted_iota(I32, s.shape, 0)
            qpos = lax.broadcasted_iota(I32, s.shape, 1) % tq
            s = jnp.where((ki // CHUNK) <= (qpos // CHUNK), s, NEG)
        if kb == 0:
            m_old = jnp.full(m_ref.shape, NEG, F32)
            l_old = jnp.zeros(l_ref.shape, F32)
            acc_old = jnp.zeros(acc_ref.shape, F32)
        else:
            m_old, l_old, acc_old = m_ref[...], l_ref[...], acc_ref[...]
        m_new = jnp.maximum(m_old, jnp.max(s, axis=0, keepdims=True))
        alpha = jnp.exp2(m_old - m_new)
        p = jnp.exp2(s - m_new)
        l_new = alpha * l_old + jnp.sum(p, axis=0, keepdims=True)
        acc_new = alpha * acc_old + _dot_tn(v_ref[kb * tq:(kb + 1) * tq, :], p.astype(BF16))
        if kb < qi:
            m_ref[...], l_ref[...], acc_ref[...] = m_new, l_new, acc_new
        else:
            o = acc_new / l_new
            o = (o[:, 0:tq] - lam * o[:, tq:2 * tq]).T
            o_ref[qi * tq:(qi + 1) * tq, :] = (_rms(o, sn_ref[...]) * (1.0 - lambda_init)).astype(o_ref.dtype)

    steps = [(qi, kb) for qi in range(nq) for kb in range(qi + 1)]
    bufs = (s0_ref, s1_ref)
    scores(steps[0], bufs[0])
    for n, step in enumerate(steps):
        if n + 1 < len(steps):
            scores(steps[n + 1], bufs[(n + 1) % 2])
        update(step, bufs[n % 2])


def diff_attention(qkv, lam_p, sub_norm, *, batch, seq, heads, tq, lambda_init):
    t = qkv.shape[0]
    return pl.pallas_call(
        functools.partial(_attn_kernel, tq=tq, lambda_init=lambda_init),
        grid=(batch, heads),
        in_specs=[
            pl.BlockSpec((seq, LANES), lambda b, h: (b, h)),
            pl.BlockSpec((seq, LANES), lambda b, h: (b, heads + h)),
            pl.BlockSpec((seq, LANES), lambda b, h: (b, 2 * heads + h)),
            pl.BlockSpec(lam_p.shape, lambda b, h: (0, 0)),
            pl.BlockSpec(sub_norm.shape, lambda b, h: (0, 0)),
        ],
        out_specs=pl.BlockSpec((seq, LANES), lambda b, h: (b, h)),
        out_shape=jax.ShapeDtypeStruct((t, heads * LANES), BF16),
        scratch_shapes=[
            pltpu.VMEM((1, 2 * tq), F32),
            pltpu.VMEM((1, 2 * tq), F32),
            pltpu.VMEM((LANES, 2 * tq), F32),
            pltpu.VMEM((tq, 2 * tq), F32),
            pltpu.VMEM((tq, 2 * tq), F32),
        ],
        compiler_params=_params("parallel", "parallel"),
        name="diff_attention",
    )(qkv, qkv, qkv, lam_p, sub_norm)


def _router_kernel(h_ref, w_ref, meta_ref, cnt_ref, run_ref):
    tm = h_ref.shape[0]

    @pl.when(pl.program_id(0) == 0)
    def _():
        run_ref[...] = jnp.zeros(run_ref.shape, F32)

    h = h_ref[...]
    w = w_ref[...]
    h_hi = h.astype(BF16)
    h_lo = (h - h_hi.astype(F32)).astype(BF16)
    w_hi = w.astype(BF16)
    w_lo = (w - w_hi.astype(F32)).astype(BF16)
    logits = _dot(h_hi, w_hi) + (_dot(h_hi, w_lo) + _dot(h_lo, w_hi))
    lane = lax.broadcasted_iota(I32, logits.shape, 1)
    lane_f = lane.astype(F32)
    logits = jnp.where(lane < N_EXPERTS, logits, NEG)
    m1 = jnp.max(logits, axis=-1, keepdims=True)
    i1 = jnp.min(jnp.where(logits == m1, lane_f, float(LANES)), axis=-1, keepdims=True)
    rest = jnp.where(lane_f == i1, NEG, logits)
    m2 = jnp.max(rest, axis=-1, keepdims=True)
    i2 = jnp.min(jnp.where(rest == m2, lane_f, float(LANES)), axis=-1, keepdims=True)
    e2 = jnp.exp(m2 - m1)
    g1 = 1.0 / (1.0 + e2)
    g2 = e2 * g1

    oh1 = lane_f == i1
    oh2 = lane_f == i2
    both = jnp.where(oh1, 1.0, 0.0) + jnp.where(oh2, 1.0, 0.0)
    ri = lax.broadcasted_iota(I32, (tm, tm), 0)
    ci = lax.broadcasted_iota(I32, (tm, tm), 1)
    strict = jnp.where(ri > ci, 1.0, 0.0).astype(BF16)
    before = run_ref[...] + _dot(strict, both.astype(BF16))
    r1 = jnp.sum(jnp.where(oh1, before, 0.0), axis=-1, keepdims=True)
    r2 = jnp.sum(jnp.where(oh2, before, 0.0), axis=-1, keepdims=True)
    run_ref[...] += jnp.sum(both, axis=0, keepdims=True)
    cnt_ref[...] = run_ref[...]

    meta = jnp.where(lane == 0, g1, jnp.where(lane == 1, g2, 0.0))
    meta = jnp.where(lane == 2, r1, jnp.where(lane == 3, r2, meta))
    meta_ref[...] = jnp.where(lane == 4, i1, jnp.where(lane == 5, i2, meta))


def router(h, w_pad, *, tm):
    t, d = h.shape
    return pl.pallas_call(
        _router_kernel,
        grid=(t // tm,),
        in_specs=[
            pl.BlockSpec((tm, d), lambda i: (i, 0)),
            pl.BlockSpec((d, LANES), lambda i: (0, 0)),
        ],
        out_specs=[
            pl.BlockSpec((tm, LANES), lambda i: (i, 0)),
            pl.BlockSpec((1, LANES), lambda i: (0, 0)),
        ],
        out_shape=[
            jax.ShapeDtypeStruct((t, LANES), F32),
            jax.ShapeDtypeStruct((1, LANES), F32),
        ],
        scratch_shapes=[pltpu.VMEM((1, LANES), F32)],
        compiler_params=_params("arbitrary"),
        name="moe_router",
    )(h, w_pad)


def _dispatch_kernel(pad_ref, dest_hbm, h_ref, xb_hbm, idx_smem, zrow, isem, sem, *, td):
    i = pl.program_id(0)
    cp = pltpu.make_async_copy(dest_hbm.at[i], idx_smem, isem)
    cp.start()
    cp.wait()

    def body(r, c):
        for k in range(2):
            pltpu.make_async_copy(h_ref.at[pl.ds(r, 1)], xb_hbm.at[pl.ds(idx_smem[2 * r + k], 1)], sem).start()
        return c

    lax.fori_loop(0, td, body, 0, unroll=8)

    @pl.when(i == 0)
    def _():
        zrow[...] = jnp.zeros(zrow.shape, zrow.dtype)
        nrange = pad_ref.shape[0] // 2
        for e in range(nrange):
            start = pad_ref[e]
            n = pad_ref[nrange + e]

            def zbody(r, c):
                pltpu.make_async_copy(zrow, xb_hbm.at[pl.ds(start + r, 1)], sem).start()
                return c

            lax.fori_loop(0, n, zbody, 0)

            def wbody(r, c):
                pltpu.make_async_copy(zrow, xb_hbm.at[pl.ds(0, 1)], sem).wait()
                return c

            lax.fori_loop(0, n, wbody, 0)

    for k in range(2):
        pltpu.make_async_copy(h_ref, xb_hbm.at[pl.ds(0, td)], sem).wait()


def moe_dispatch(h, dest, pad_info, *, cap, td):
    t, d = h.shape
    return pl.pallas_call(
        functools.partial(_dispatch_kernel, td=td),
        grid_spec=pltpu.PrefetchScalarGridSpec(
            num_scalar_prefetch=1,
            grid=(t // td,),
            in_specs=[
                pl.BlockSpec(memory_space=pl.ANY),
                pl.BlockSpec((td, d), lambda i, pad: (i, 0)),
            ],
            out_specs=pl.BlockSpec(memory_space=pl.ANY),
            scratch_shapes=[
                pltpu.SMEM((2 * td,), I32),
                pltpu.VMEM((1, d), h.dtype),
                pltpu.SemaphoreType.DMA,
                pltpu.SemaphoreType.DMA,
            ],
        ),
        out_shape=jax.ShapeDtypeStruct((cap, d), h.dtype),
        compiler_params=_params("arbitrary"),
        name="moe_dispatch",
    )(pad_info, dest, h)


def _experts_kernel(be_ref, x_ref, wg_ref, wu_ref, wd_ref, o_ref, xb_ref):
    i = pl.program_id(0)
    j = pl.program_id(1)
    used = i < be_ref[be_ref.shape[0] - 1]

    @pl.when(j == 0)
    def _():
        o_ref[...] = jnp.zeros(o_ref.shape, o_ref.dtype)

    @pl.when(used & (j == 0))
    def _():
        xb_ref[...] = x_ref[...].astype(BF16)

    @pl.when(used)
    def _():
        _swiglu_accumulate(xb_ref, wg_ref[0], wu_ref[0], wd_ref[0], o_ref)


def moe_experts(block_e, xb, wg, wu, wd, *, bm, tf):
    cap, d = xb.shape
    f = wg.shape[2]
    nblk = cap // bm
    return pl.pallas_call(
        _experts_kernel,
        grid_spec=pltpu.PrefetchScalarGridSpec(
            num_scalar_prefetch=1,
            grid=(nblk, f // tf),
            in_specs=[
                pl.BlockSpec((bm, d), lambda i, j, be: (jnp.minimum(i, be[nblk] - 1), 0)),
                pl.BlockSpec((1, d, tf), lambda i, j, be: (be[i], 0, j)),
                pl.BlockSpec((1, d, tf), lambda i, j, be: (be[i], 0, j)),
                pl.BlockSpec((1, tf, d), lambda i, j, be: (be[i], j, 0)),
            ],
            out_specs=pl.BlockSpec((bm, d), lambda i, j, be: (i, 0)),
            scratch_shapes=[pltpu.VMEM((bm, d), BF16)],
        ),
        out_shape=jax.ShapeDtypeStruct((cap, d), F32),
        compiler_params=_params("parallel", "arbitrary"),
        name="moe_experts",
    )(block_e, xb, wg, wu, wd)


def _combine_kernel(dest_hbm, yb_hbm, x_ref, gate_ref, o_ref, idx_smem, y0, y1, isem, sem, *, tc):
    i = pl.program_id(0)
    cp = pltpu.make_async_copy(dest_hbm.at[i], idx_smem, isem)
    cp.start()
    cp.wait()

    def body(r, c):
        pltpu.make_async_copy(yb_hbm.at[pl.ds(idx_smem[2 * r], 1)], y0.at[pl.ds(r, 1)], sem).start()
        pltpu.make_async_copy(yb_hbm.at[pl.ds(idx_smem[2 * r + 1], 1)], y1.at[pl.ds(r, 1)], sem).start()
        return c

    lax.fori_loop(0, tc, body, 0, unroll=8)
    pltpu.make_async_copy(yb_hbm.at[pl.ds(0, tc)], y0, sem).wait()
    pltpu.make_async_copy(yb_hbm.at[pl.ds(0, tc)], y1, sem).wait()
    gate = gate_ref[...]
    o_ref[...] = x_ref[...] + (gate[:, 0:1] * y0[...] + gate[:, 1:2] * y1[...])


def moe_combine(dest, yb, x, gates, *, tc):
    t, d = x.shape
    return pl.pallas_call(
        functools.partial(_combine_kernel, tc=tc),
        grid=(t // tc,),
        in_specs=[
            pl.BlockSpec(memory_space=pl.ANY),
            pl.BlockSpec(memory_space=pl.ANY),
            pl.BlockSpec((tc, d), lambda i: (i, 0)),
            pl.BlockSpec((tc, LANES), lambda i: (i, 0)),
        ],
        out_specs=pl.BlockSpec((tc, d), lambda i: (i, 0)),
        out_shape=jax.ShapeDtypeStruct((t, d), F32),
        scratch_shapes=[
            pltpu.SMEM((2 * tc,), I32),
            pltpu.VMEM((tc, d), F32),
            pltpu.VMEM((tc, d), F32),
            pltpu.SemaphoreType.DMA,
            pltpu.SemaphoreType.DMA,
        ],
        compiler_params=_params("arbitrary"),
        name="moe_combine",
    )(dest, yb, x, gates)


def _tile(n, pref):
    t = min(n, pref)
    while n % t:
        t //= 2
    return t


def even_layer(x, p, *, batch, seq):
    t, d = x.shape
    tm = _tile(t, 1024)
    proj, dt_raw = norm_proj(x, p["norm_mix"], p["w_in_main"], p["w_in_dt"], tm=tm, tn=p["tn_in"])
    mix = conv_ssd_mixer(proj, dt_raw, p["conv_w"], p["xw"], p["bcw"], p["xb"], p["bcb"], p["dt_bias"],
                         p["a_log"], p["d_skip"], p["ssd_norm_g"], p["expand"],
                         batch=batch, seq=seq, blk=_tile(seq, 256))
    x, h = proj_res_norm(mix, p["w_out"], x, p["norm_ffn"], tm=_tile(t, 512), h_dtype=BF16)
    return ffn(h, x, p["w_gate"], p["w_up"], p["w_down"], tm=tm, tf=FF_TILE)


def moe_layer(x, h, p, *, bm):
    t, d = x.shape
    tr = _tile(t, 512)
    meta, counts = router(h, p["w_router"], tm=tr)
    rank = meta[:, 2:4].astype(I32)
    top_e = meta[:, 4:6].astype(I32)
    counts = counts[0, :N_EXPERTS].astype(I32)
    padded = (counts + bm - 1) // bm * bm
    pends = jnp.cumsum(padded)
    pstarts = pends - padded
    dest = pstarts[top_e] + rank
    nblk = (2 * t) // bm + N_EXPERTS
    cap = nblk * bm
    block_e = jnp.minimum(jnp.searchsorted(pends, jnp.arange(nblk, dtype=I32) * bm, side="right"),
                          N_EXPERTS - 1).astype(I32)
    block_e = jnp.concatenate([block_e, (pends[-1:] // bm).astype(I32)])
    pad_info = jnp.concatenate([pstarts + counts, pends[-1:], padded - counts, cap - pends[-1:]]).astype(I32)
    td = _tile(t, 512)
    dest_t = dest.reshape(t // td, 2 * td)
    xb = moe_dispatch(h, dest_t, pad_info, cap=cap, td=td)
    yb = moe_experts(block_e, xb, p["w_gate"], p["w_up"], p["w_down"], bm=bm, tf=FF_TILE)
    return moe_combine(dest_t, yb, x, meta, tc=td)


def odd_layer(x, p, *, batch, seq, lambda_init, cos_t, sin_t):
    t, d = x.shape
    tm = _tile(seq, 1024)
    qkv = qkv_proj(x, p["norm_mix"], p["w_qkv"], p["head_gain"], cos_t, sin_t, p["pavg"], tm=tm, seq=seq)
    o = diff_attention(qkv, p["lam"], p["sub_norm"], batch=batch, seq=seq, heads=d // LANES,
                       tq=_tile(seq, 512), lambda_init=lambda_init)
    x, h = proj_res_norm(o, p["w_o"], x, p["norm_ffn"], tm=_tile(t, 512), h_dtype=F32)
    return moe_layer(x, h, p, bm=_tile(t, 512))


def _rope_tables(seq):
    inv = 1.0 / (ROPE_THETA ** (jnp.arange(0, HEAD, 2, dtype=F32) / HEAD))
    ang = jnp.arange(seq, dtype=F32)[:, None] * inv[None, :]
    cos, sin = jnp.cos(ang), jnp.sin(ang)
    cos_t = jnp.concatenate([cos, cos, cos, cos], axis=1)
    sin_t = jnp.concatenate([-sin, sin, -sin, sin], axis=1)
    return cos_t, sin_t


def kernel(x, even_norm_mix, even_w_in, even_conv_w, ssd_conv_w, ssd_conv_b, ssd_dt_bias, ssd_a_log, ssd_d, ssd_norm_g, even_w_out, even_norm_ffn, ffn_w_gate, ffn_w_up, ffn_w_down, odd_norm_mix, attn_w_qkv, attn_q_norm, attn_k_norm, attn_lambda_q1, attn_lambda_k1, attn_lambda_q2, attn_lambda_k2, attn_sub_norm, attn_w_o, odd_norm_ffn, moe_w_router, moe_w_gate, moe_w_up, moe_w_down):
    batch, seq, d = x.shape
    depth = even_norm_mix.shape[0] + odd_norm_mix.shape[0]
    n_main = 5 * d + 2 * LANES
    heads = jnp.arange(d) // HEAD
    expand = (jnp.arange(LANES)[:, None] == heads[None, :]).astype(BF16)
    pavg = jnp.where(heads[:, None] == heads[None, :], 1.0 / HEAD, 0.0).astype(BF16)
    cos_t, sin_t = _rope_tables(seq)
    scale = math.log2(math.e) / math.sqrt(HEAD)

    def pad_lanes(v):
        return jnp.pad(v.astype(F32), (0, LANES - v.shape[0]))[None, :]

    xf = x.reshape(batch * seq, d)
    for layer in range(depth):
        i = layer // 2
        if layer % 2 == 0:
            w_in = even_w_in[i]
            p = dict(
                norm_mix=even_norm_mix[i][None, :],
                w_in_main=w_in[:, :n_main].astype(BF16),
                w_in_dt=jnp.pad(w_in[:, n_main:], ((0, 0), (0, LANES - (w_in.shape[1] - n_main)))).astype(BF16),
                tn_in=n_main // 3,
                conv_w=even_conv_w[i],
                xw=ssd_conv_w[i][:, :d], bcw=ssd_conv_w[i][:, d:],
                xb=ssd_conv_b[i][None, :d], bcb=ssd_conv_b[i][None, d:],
                dt_bias=pad_lanes(ssd_dt_bias[i]), a_log=pad_lanes(ssd_a_log[i]),
                d_skip=jnp.repeat(ssd_d[i].astype(F32), HEAD)[None, :],
                ssd_norm_g=ssd_norm_g[i][None, :], expand=expand,
                w_out=even_w_out[i].astype(BF16), norm_ffn=even_norm_ffn[i][None, :],
                w_gate=ffn_w_gate[i].astype(BF16), w_up=ffn_w_up[i].astype(BF16),
                w_down=ffn_w_down[i].astype(BF16),
            )
            xf = even_layer(xf, p, batch=batch, seq=seq)
        else:
            lambda_init = 0.8 - 0.6 * math.exp(-0.3 * layer)
            reps = d // HEAD
            head_gain = jnp.stack([jnp.tile(attn_q_norm[i], reps) * scale, jnp.tile(attn_k_norm[i], reps)])[:, None, :]
            lam = jnp.stack([pad_lanes(v)[0] for v in
                             (attn_lambda_q1[i], attn_lambda_k1[i], attn_lambda_q2[i], attn_lambda_k2[i])])
            p = dict(
                norm_mix=odd_norm_mix[i][None, :], w_qkv=attn_w_qkv[i].astype(BF16),
                head_gain=head_gain.astype(F32), pavg=pavg, lam=lam,
                sub_norm=attn_sub_norm[i][None, :], w_o=attn_w_o[i].astype(BF16),
                norm_ffn=odd_norm_ffn[i][None, :],
                w_router=jnp.pad(moe_w_router[i], ((0, 0), (0, LANES - N_EXPERTS))),
                w_gate=moe_w_gate[i].astype(BF16), w_up=moe_w_up[i].astype(BF16),
                w_down=moe_w_down[i].astype(BF16),
            )
            xf = odd_layer(xf, p, batch=batch, seq=seq, lambda_init=lambda_init, cos_t=cos_t, sin_t=sin_t)
    return xf.reshape(batch, seq, d)
```

```python
import functools
import math

import jax
import jax.numpy as jnp
from jax import lax
from jax.experimental import pallas as pl
from jax.experimental.pallas import tpu as pltpu

F32 = jnp.float32
BF16 = jnp.bfloat16
I32 = jnp.int32

EPS = 1e-6
ROPE_THETA = 10000.0
CHUNK = 64
HEAD = 64
LANES = 128
SSD_HEADS = 16
SSD_GROUPS = 2
N_EXPERTS = 8
FF_TILE = 1408
VMEM_LIMIT = 56 * 1024 * 1024
NEG = -1e30


def _params(*sem):
    return pltpu.CompilerParams(dimension_semantics=sem, vmem_limit_bytes=VMEM_LIMIT)


def _rms(x, g):
    ms = jnp.mean(x * x, axis=-1, keepdims=True)
    return x * lax.rsqrt(ms + EPS) * g


def _silu(x):
    return x * (1.0 / (1.0 + jnp.exp(-x)))


def _dot(a, b):
    return jnp.dot(a, b, preferred_element_type=F32)


def _dot_nt(a, b):
    return lax.dot_general(a, b, (((1,), (1,)), ((), ())), preferred_element_type=F32)


def _dot_tn(a, b):
    return lax.dot_general(a, b, (((0,), (0,)), ((), ())), preferred_element_type=F32)


def _norm_proj_kernel(x_ref, g_ref, w_ref, wdt_ref, o_ref, odt_ref, h_ref):
    @pl.when(pl.program_id(1) == 0)
    def _():
        h = _rms(x_ref[...], g_ref[...]).astype(BF16)
        h_ref[...] = h
        odt_ref[...] = _dot(h, wdt_ref[...])

    o_ref[...] = _dot(h_ref[...], w_ref[...]).astype(o_ref.dtype)


def norm_proj(x, g, w_main, w_dt, *, tm, tn):
    t, d = x.shape
    n = w_main.shape[1]
    return pl.pallas_call(
        _norm_proj_kernel,
        grid=(t // tm, n // tn),
        in_specs=[
            pl.BlockSpec((tm, d), lambda i, j: (i, 0)),
            pl.BlockSpec((1, d), lambda i, j: (0, 0)),
            pl.BlockSpec((d, tn), lambda i, j: (0, j)),
            pl.BlockSpec((d, LANES), lambda i, j: (0, 0)),
        ],
        out_specs=[
            pl.BlockSpec((tm, tn), lambda i, j: (i, j)),
            pl.BlockSpec((tm, LANES), lambda i, j: (i, 0)),
        ],
        out_shape=[
            jax.ShapeDtypeStruct((t, n), BF16),
            jax.ShapeDtypeStruct((t, LANES), F32),
        ],
        scratch_shapes=[pltpu.VMEM((tm, d), BF16)],
        compiler_params=_params("parallel", "arbitrary"),
        name="norm_proj",
    )(x, g, w_main, w_dt)


HALO = 8


def _split3(v):
    hi = v.astype(BF16)
    r = v - hi.astype(F32)
    mid = r.astype(BF16)
    lo = (r - mid.astype(F32)).astype(BF16)
    return hi, mid, lo


def _expand(v, e, terms):
    parts = _split3(v)[:terms]
    out = _dot(parts[0], e)
    for p in parts[1:]:
        out = out + _dot(p, e)
    return out


def _mixer_kernel(bg_ref, cg_ref, hv_ref, z_ref, xs_ref, bc_ref, dtr_ref,
                  cw_ref, xw_ref, bcw_ref, xb_ref, bcb_ref, dtb_ref, alog_ref,
                  dskip_ref, ng_ref, e_ref,
                  o_ref, ubuf, xbuf, bcbuf, state, ybuf):
    L = o_ref.shape[0]
    d = bg_ref.shape[1]
    nchunk = L // CHUNK

    @pl.when(pl.program_id(1) == 0)
    def _():
        for buf in (ubuf, xbuf, bcbuf):
            buf[:, 0:HALO, :] = jnp.zeros((buf.shape[0], HALO, LANES), F32)
        state[...] = jnp.zeros(state.shape, F32)

    def causal_conv(v, buf, w):
        taps = w.shape[0]
        cols = []
        for c in range(v.shape[1] // LANES):
            lanes = slice(c * LANES, (c + 1) * LANES)
            vc = v[:, lanes]
            buf[c, HALO:HALO + L, :] = vc
            acc = w[taps - 1:taps, lanes] * vc
            for k in range(taps - 1):
                off = HALO - (taps - 1) + k
                acc = acc + w[k:k + 1, lanes] * buf[c, off:off + L, :]
            buf[c, 0:HALO, :] = buf[c, L:L + HALO, :]
            cols.append(acc)
        return jnp.concatenate(cols, axis=1)

    u = cg_ref[...].astype(F32) * hv_ref[...].astype(F32)
    conv = causal_conv(u, ubuf, cw_ref[...])
    o_ref[:, 0:d] = (bg_ref[...].astype(F32) * conv).astype(o_ref.dtype)

    xc = _silu(causal_conv(xs_ref[...].astype(F32), xbuf, xw_ref[...]) + xb_ref[...])
    bcc = _silu(causal_conv(bc_ref[...].astype(F32), bcbuf, bcw_ref[...]) + bcb_ref[...])

    e = e_ref[...]
    dtv = dtr_ref[...] + dtb_ref[...]
    dt = jnp.maximum(dtv, 0.0) + jnp.log(1.0 + jnp.exp(-jnp.abs(dtv)))
    a = -jnp.exp(alog_ref[...])
    da = dt * a

    ri = lax.broadcasted_iota(I32, (L, L), 0)
    ci = lax.broadcasted_iota(I32, (L, L), 1)
    tri = jnp.where((ri >= ci) & ((ri // CHUNK) == (ci // CHUNK)), 1.0, 0.0).astype(BF16)
    d_hi, d_mid, d_lo = _split3(da)
    acs = _dot(tri, d_hi) + _dot(tri, d_mid) + _dot(tri, d_lo)

    last = jnp.concatenate(
        [jnp.broadcast_to(acs[c * CHUNK + CHUNK - 1:c * CHUNK + CHUNK, :], (CHUNK, LANES))
         for c in range(nchunk)], axis=0)
    col_all = _expand(acs, e, 3)
    e_all = jnp.exp(col_all)
    ds_all = jnp.exp(_expand(last, e, 3) - col_all)
    xdt = xc * _expand(dt, e, 3)
    xwt = (xdt * ds_all).astype(BF16)
    xdt_b = xdt.astype(BF16)

    lane = lax.broadcasted_iota(I32, (CHUNK, LANES), 1)
    row = lax.broadcasted_iota(I32, (CHUNK, LANES), 0)
    low_half = lane < HEAD
    tri2 = row >= (lane % HEAD)
    lane2 = lax.broadcasted_iota(I32, (2 * CHUNK, LANES), 1)
    row2 = lax.broadcasted_iota(I32, (2 * CHUNK, LANES), 0)
    bd_mask = (lane2 < HEAD) == (row2 < CHUNK)
    srow = lax.broadcasted_iota(I32, state.shape, 0)
    scol = lax.broadcasted_iota(I32, state.shape, 1)
    state_mask = (srow // HEAD) == (scol // (d // SSD_GROUPS))

    for c in range(nchunk):
        r0 = c * CHUNK
        bm = bcc[r0:r0 + CHUNK, 0:LANES]
        cm = bcc[r0:r0 + CHUNK, LANES:2 * LANES]
        bm_b = bm.astype(BF16)
        cm_b = cm.astype(BF16)
        acs_c = acs[r0:r0 + CHUNK, :]
        acs_t = jnp.concatenate([acs_c, acs_c], axis=0).T
        st = state[...]
        y_off = _dot(cm_b, st.astype(BF16)) * e_all[r0:r0 + CHUNK, :]
        bm2 = jnp.concatenate([bm_b, bm_b], axis=0)
        cb2 = []
        for g in range(SSD_GROUPS):
            cmm = jnp.where((lane // HEAD) == g, cm, 0.0).astype(BF16)
            cb2.append(_dot_nt(cmm, bm2))
        yd = []
        for q in range(SSD_HEADS // 2):
            colq = col_all[r0:r0 + CHUNK, q * LANES:(q + 1) * LANES]
            rowq = jnp.where(low_half[0:1], acs_t[2 * q:2 * q + 1, :], acs_t[2 * q + 1:2 * q + 2, :])
            dec = jnp.exp(jnp.where(tri2, colq - rowq, NEG))
            m2 = (cb2[q // (SSD_HEADS // 2 // SSD_GROUPS)] * dec).astype(BF16)
            xq = xdt_b[r0:r0 + CHUNK, q * LANES:(q + 1) * LANES]
            xq2 = jnp.concatenate([xq, xq], axis=0)
            xbd = jnp.where(bd_mask, xq2, jnp.zeros_like(xq2))
            yd.append(_dot(m2, xbd))
        y = jnp.concatenate(yd, axis=1) + y_off + dskip_ref[...] * xc[r0:r0 + CHUNK, :]
        ybuf[r0:r0 + CHUNK, :] = y
        s_new = _dot_tn(bm_b, xwt[r0:r0 + CHUNK, :])
        state[...] = st * e_all[r0 + CHUNK - 1:r0 + CHUNK, :] + jnp.where(state_mask, s_new, 0.0)

    y = ybuf[...] * _silu(z_ref[...].astype(F32))
    half = d // SSD_GROUPS
    ng = ng_ref[...]
    for g in range(SSD_GROUPS):
        yg = y[:, g * half:(g + 1) * half]
        o_ref[:, d + g * half:d + (g + 1) * half] = _rms(yg, ng[:, g * half:(g + 1) * half]).astype(o_ref.dtype)


def conv_ssd_mixer(proj, dt_raw, cw, xw, bcw, xb, bcb, dtb, alog, dskip, ng, expand, *, batch, seq, blk):
    t = proj.shape[0]
    d = 1024
    nb = seq // blk
    row = lambda b, s: b * nb + s
    full = lambda shape: pl.BlockSpec(shape, lambda b, s: (0, 0))
    return pl.pallas_call(
        _mixer_kernel,
        grid=(batch, nb),
        in_specs=[
            pl.BlockSpec((blk, d), lambda b, s: (row(b, s), 0)),
            pl.BlockSpec((blk, d), lambda b, s: (row(b, s), 1)),
            pl.BlockSpec((blk, d), lambda b, s: (row(b, s), 2)),
            pl.BlockSpec((blk, d), lambda b, s: (row(b, s), 3)),
            pl.BlockSpec((blk, d), lambda b, s: (row(b, s), 4)),
            pl.BlockSpec((blk, 2 * LANES), lambda b, s: (row(b, s), 5 * d // (2 * LANES))),
            pl.BlockSpec((blk, LANES), lambda b, s: (row(b, s), 0)),
            full(cw.shape), full(xw.shape), full(bcw.shape), full(xb.shape), full(bcb.shape),
            full(dtb.shape), full(alog.shape), full(dskip.shape), full(ng.shape), full(expand.shape),
        ],
        out_specs=pl.BlockSpec((blk, 2 * d), lambda b, s: (row(b, s), 0)),
        out_shape=jax.ShapeDtypeStruct((t, 2 * d), BF16),
        scratch_shapes=[
            pltpu.VMEM((d // LANES, blk + HALO, LANES), F32),
            pltpu.VMEM((d // LANES, blk + HALO, LANES), F32),
            pltpu.VMEM((2, blk + HALO, LANES), F32),
            pltpu.VMEM((SSD_GROUPS * HEAD, d), F32),
            pltpu.VMEM((blk, d), F32),
        ],
        compiler_params=_params("parallel", "arbitrary"),
        name="conv_ssd_mixer",
    )(proj, proj, proj, proj, proj, proj, dt_raw, cw, xw, bcw, xb, bcb, dtb, alog, dskip, ng, expand)


def _proj_res_norm_kernel(a_ref, w_ref, x_ref, g_ref, xo_ref, ho_ref):
    y = x_ref[...] + _dot(a_ref[...], w_ref[...])
    xo_ref[...] = y
    ho_ref[...] = _rms(y, g_ref[...]).astype(ho_ref.dtype)


def proj_res_norm(a, w, x, g, *, tm):
    t, k = a.shape
    d = w.shape[1]
    return pl.pallas_call(
        _proj_res_norm_kernel,
        grid=(t // tm,),
        in_specs=[
            pl.BlockSpec((tm, k), lambda i: (i, 0)),
            pl.BlockSpec((k, d), lambda i: (0, 0)),
            pl.BlockSpec((tm, d), lambda i: (i, 0)),
            pl.BlockSpec((1, d), lambda i: (0, 0)),
        ],
        out_specs=[
            pl.BlockSpec((tm, d), lambda i: (i, 0)),
            pl.BlockSpec((tm, d), lambda i: (i, 0)),
        ],
        out_shape=[
            jax.ShapeDtypeStruct((t, d), F32),
            jax.ShapeDtypeStruct((t, d), BF16),
        ],
        compiler_params=_params("parallel"),
        name="proj_res_norm",
    )(a, w, x, g)


SWIGLU_ROWS = 512


def _swiglu_accumulate(x_ref, wg, wu, wd, o_ref):
    rows = min(SWIGLU_ROWS, x_ref.shape[0])
    for r0 in range(0, x_ref.shape[0], rows):
        x = x_ref[r0:r0 + rows, :]
        g = _dot(x, wg)
        u = _dot(x, wu)
        o_ref[r0:r0 + rows, :] += _dot((_silu(g) * u).astype(BF16), wd)


def _ffn_kernel(h_ref, x_ref, wg_ref, wu_ref, wd_ref, o_ref):
    @pl.when(pl.program_id(1) == 0)
    def _():
        o_ref[...] = x_ref[...]

    _swiglu_accumulate(h_ref, wg_ref[...], wu_ref[...], wd_ref[...], o_ref)


def ffn(h, x, wg, wu, wd, *, tm, tf):
    t, d = x.shape
    f = wg.shape[1]
    return pl.pallas_call(
        _ffn_kernel,
        grid=(t // tm, f // tf),
        in_specs=[
            pl.BlockSpec((tm, d), lambda i, j: (i, 0)),
            pl.BlockSpec((tm, d), lambda i, j: (i, 0)),
            pl.BlockSpec((d, tf), lambda i, j: (0, j)),
            pl.BlockSpec((d, tf), lambda i, j: (0, j)),
            pl.BlockSpec((tf, d), lambda i, j: (j, 0)),
        ],
        out_specs=pl.BlockSpec((tm, d), lambda i, j: (i, 0)),
        out_shape=jax.ShapeDtypeStruct((t, d), F32),
        compiler_params=_params("parallel", "arbitrary"),
        name="ffn",
    )(h, x, wg, wu, wd)


def _qkv_kernel(x_ref, g_ref, w_ref, hg_ref, cos_ref, sin_ref, p_ref, o_ref, h_ref):
    j = pl.program_id(1)

    @pl.when(j == 0)
    def _():
        h_ref[...] = _rms(x_ref[...], g_ref[...]).astype(BF16)

    y = _dot(h_ref[...], w_ref[...])

    @pl.when(j < 2)
    def _():
        d = y.shape[1]
        ms = _dot((y * y).astype(BF16), p_ref[...])
        yn = y * lax.rsqrt(ms + EPS) * hg_ref[0]
        lane = lax.broadcasted_iota(I32, y.shape, 1)
        partner = jnp.where((lane % HEAD) < HEAD // 2,
                            pltpu.roll(yn, d - HEAD // 2, 1), pltpu.roll(yn, HEAD // 2, 1))
        reps = d // LANES
        cos = jnp.concatenate([cos_ref[...]] * reps, axis=1)
        sin = jnp.concatenate([sin_ref[...]] * reps, axis=1)
        o_ref[...] = (yn * cos + partner * sin).astype(o_ref.dtype)

    @pl.when(j == 2)
    def _():
        o_ref[...] = y.astype(o_ref.dtype)


def qkv_proj(x, g, w, head_gain, cos_t, sin_t, pavg, *, tm, seq):
    t, d = x.shape
    ns = seq // tm
    return pl.pallas_call(
        _qkv_kernel,
        grid=(t // tm, 3),
        in_specs=[
            pl.BlockSpec((tm, d), lambda i, j: (i, 0)),
            pl.BlockSpec((1, d), lambda i, j: (0, 0)),
            pl.BlockSpec((d, d), lambda i, j: (0, j)),
            pl.BlockSpec((1, 1, d), lambda i, j: (jnp.minimum(j, 1), 0, 0)),
            pl.BlockSpec((tm, LANES), lambda i, j: (i % ns, 0)),
            pl.BlockSpec((tm, LANES), lambda i, j: (i % ns, 0)),
            pl.BlockSpec((d, d), lambda i, j: (0, 0)),
        ],
        out_specs=pl.BlockSpec((tm, d), lambda i, j: (i, j)),
        out_shape=jax.ShapeDtypeStruct((t, 3 * d), BF16),
        scratch_shapes=[pltpu.VMEM((tm, d), BF16)],
        compiler_params=_params("parallel", "arbitrary"),
        name="qkv_proj",
    )(x, g, w, head_gain, cos_t, sin_t, pavg)


def _attn_kernel(q_ref, k_ref, v_ref, lam_ref, sn_ref, o_ref, m_ref, l_ref, acc_ref, s0_ref, s1_ref, *,
                 tq, lambda_init):
    seq = q_ref.shape[0]
    nq = seq // tq
    lane = lax.broadcasted_iota(I32, (tq, LANES), 1)
    lam_p = lam_ref[...]
    lam = (jnp.exp(jnp.sum(lam_p[0:1] * lam_p[1:2], axis=-1, keepdims=True))
           - jnp.exp(jnp.sum(lam_p[2:3] * lam_p[3:4], axis=-1, keepdims=True)) + lambda_init)

    def stacked_q(qi):
        q = q_ref[qi * tq:(qi + 1) * tq, :]
        zero = jnp.zeros_like(q)
        return jnp.concatenate([jnp.where(lane < HEAD, q, zero), jnp.where(lane >= HEAD, q, zero)], axis=0)

    def scores(step, dst):
        qi, kb = step
        dst[...] = _dot_nt(k_ref[kb * tq:(kb + 1) * tq, :], stacked_q(qi))

    def update(step, src):
        qi, kb = step
        s = src[...]
        if kb == qi:
            ki = lax.broadcasted_iota(I32, s.shape, 0)
            qpos = lax.broadcasted_iota(I32, s.shape, 1) % tq
            s = jnp.where((ki // CHUNK) <= (qpos // CHUNK), s, NEG)
        if kb == 0:
            m_old = jnp.full(m_ref.shape, NEG, F32)
            l_old = jnp.zeros(l_ref.shape, F32)
            acc_old = jnp.zeros(acc_ref.shape, F32)
        else:
            m_old, l_old, acc_old = m_ref[...], l_ref[...], acc_ref[...]
        m_new = jnp.maximum(m_old, jnp.max(s, axis=0, keepdims=True))
        alpha = jnp.exp2(m_old - m_new)
        p = jnp.exp2(s - m_new)
        l_new = alpha * l_old + jnp.sum(p, axis=0, keepdims=True)
        acc_new = alpha * acc_old + _dot_tn(v_ref[kb * tq:(kb + 1) * tq, :], p.astype(BF16))
        if kb < qi:
            m_ref[...], l_ref[...], acc_ref[...] = m_new, l_new, acc_new
        else:
            o = acc_new / l_new
            o = (o[:, 0:tq] - lam * o[:, tq:2 * tq]).T
            o_ref[qi * tq:(qi + 1) * tq, :] = (_rms(o, sn_ref[...]) * (1.0 - lambda_init)).astype(o_ref.dtype)

    steps = [(qi, kb) for qi in range(nq) for kb in range(qi + 1)]
    bufs = (s0_ref, s1_ref)
    scores(steps[0], bufs[0])
    for n, step in enumerate(steps):
        if n + 1 < len(steps):
            scores(steps[n + 1], bufs[(n + 1) % 2])
        update(step, bufs[n % 2])


def diff_attention(qkv, lam_p, sub_norm, *, batch, seq, heads, tq, lambda_init):
    t = qkv.shape[0]
    return pl.pallas_call(
        functools.partial(_attn_kernel, tq=tq, lambda_init=lambda_init),
        grid=(batch, heads),
        in_specs=[
            pl.BlockSpec((seq, LANES), lambda b, h: (b, h)),
            pl.BlockSpec((seq, LANES), lambda b, h: (b, heads + h)),
            pl.BlockSpec((seq, LANES), lambda b, h: (b, 2 * heads + h)),
            pl.BlockSpec(lam_p.shape, lambda b, h: (0, 0)),
            pl.BlockSpec(sub_norm.shape, lambda b, h: (0, 0)),
        ],
        out_specs=pl.BlockSpec((seq, LANES), lambda b, h: (b, h)),
        out_shape=jax.ShapeDtypeStruct((t, heads * LANES), BF16),
        scratch_shapes=[
            pltpu.VMEM((1, 2 * tq), F32),
            pltpu.VMEM((1, 2 * tq), F32),
            pltpu.VMEM((LANES, 2 * tq), F32),
            pltpu.VMEM((tq, 2 * tq), F32),
            pltpu.VMEM((tq, 2 * tq), F32),
        ],
        compiler_params=_params("parallel", "parallel"),
        name="diff_attention",
    )(qkv, qkv, qkv, lam_p, sub_norm)


def _route(h, w, meta_ref, cnt_ref, run_ref):
    tm = h.shape[0]
    h_hi = h.astype(BF16)
    h_lo = (h - h_hi.astype(F32)).astype(BF16)
    w_hi = w.astype(BF16)
    w_lo = (w - w_hi.astype(F32)).astype(BF16)
    logits = _dot(h_hi, w_hi) + (_dot(h_hi, w_lo) + _dot(h_lo, w_hi))
    lane = lax.broadcasted_iota(I32, logits.shape, 1)
    lane_f = lane.astype(F32)
    logits = jnp.where(lane < N_EXPERTS, logits, NEG)
    m1 = jnp.max(logits, axis=-1, keepdims=True)
    i1 = jnp.min(jnp.where(logits == m1, lane_f, float(LANES)), axis=-1, keepdims=True)
    rest = jnp.where(lane_f == i1, NEG, logits)
    m2 = jnp.max(rest, axis=-1, keepdims=True)
    i2 = jnp.min(jnp.where(rest == m2, lane_f, float(LANES)), axis=-1, keepdims=True)
    e2 = jnp.exp(m2 - m1)
    g1 = 1.0 / (1.0 + e2)
    g2 = e2 * g1

    oh1 = lane_f == i1
    oh2 = lane_f == i2
    both = jnp.where(oh1, 1.0, 0.0) + jnp.where(oh2, 1.0, 0.0)
    ri = lax.broadcasted_iota(I32, (tm, tm), 0)
    ci = lax.broadcasted_iota(I32, (tm, tm), 1)
    strict = jnp.where(ri > ci, 1.0, 0.0).astype(BF16)
    before = run_ref[...] + _dot(strict, both.astype(BF16))
    r1 = jnp.sum(jnp.where(oh1, before, 0.0), axis=-1, keepdims=True)
    r2 = jnp.sum(jnp.where(oh2, before, 0.0), axis=-1, keepdims=True)
    run_ref[...] += jnp.sum(both, axis=0, keepdims=True)
    cnt_ref[...] = run_ref[...]

    meta = jnp.where(lane == 0, g1, jnp.where(lane == 1, g2, 0.0))
    meta = jnp.where(lane == 2, r1, jnp.where(lane == 3, r2, meta))
    meta_ref[...] = jnp.where(lane == 4, i1, jnp.where(lane == 5, i2, meta))


def _proj_res_norm_route_kernel(a_ref, w_ref, x_ref, g_ref, wr_ref, xo_ref, ho_ref, meta_ref, cnt_ref, run_ref):
    @pl.when(pl.program_id(0) == 0)
    def _():
        run_ref[...] = jnp.zeros(run_ref.shape, F32)

    y = x_ref[...] + _dot(a_ref[...], w_ref[...])
    xo_ref[...] = y
    h = _rms(y, g_ref[...])
    ho_ref[...] = h
    _route(h, wr_ref[...], meta_ref, cnt_ref, run_ref)


def proj_res_norm_route(a, w, x, g, w_router, *, tm):
    t, k = a.shape
    d = w.shape[1]
    return pl.pallas_call(
        _proj_res_norm_route_kernel,
        grid=(t // tm,),
        in_specs=[
            pl.BlockSpec((tm, k), lambda i: (i, 0)),
            pl.BlockSpec((k, d), lambda i: (0, 0)),
            pl.BlockSpec((tm, d), lambda i: (i, 0)),
            pl.BlockSpec((1, d), lambda i: (0, 0)),
            pl.BlockSpec((d, LANES), lambda i: (0, 0)),
        ],
        out_specs=[
            pl.BlockSpec((tm, d), lambda i: (i, 0)),
            pl.BlockSpec((tm, d), lambda i: (i, 0)),
            pl.BlockSpec((tm, LANES), lambda i: (i, 0)),
            pl.BlockSpec((1, LANES), lambda i: (0, 0)),
        ],
        out_shape=[
            jax.ShapeDtypeStruct((t, d), F32),
            jax.ShapeDtypeStruct((t, d), F32),
            jax.ShapeDtypeStruct((t, LANES), F32),
            jax.ShapeDtypeStruct((1, LANES), F32),
        ],
        scratch_shapes=[pltpu.VMEM((1, LANES), F32)],
        compiler_params=_params("arbitrary"),
        name="proj_res_norm_route",
    )(a, w, x, g, w_router)


def _index_copy(dest_hbm, idx_smem, isem, tile, slot):
    return pltpu.make_async_copy(dest_hbm.at[tile], idx_smem.at[slot], isem.at[slot])


def _dispatch_kernel(pad_ref, dest_hbm, h_ref, xb_hbm, idx_smem, zrow, isem, sem, *, td):
    i = pl.program_id(0)
    slot = i % 2

    @pl.when(i == 0)
    def _():
        _index_copy(dest_hbm, idx_smem, isem, 0, 0).start()

    @pl.when(i + 1 < pl.num_programs(0))
    def _():
        _index_copy(dest_hbm, idx_smem, isem, i + 1, 1 - slot).start()

    _index_copy(dest_hbm, idx_smem, isem, i, slot).wait()

    def body(r, c):
        for k in range(2):
            pltpu.make_async_copy(h_ref.at[pl.ds(r, 1)], xb_hbm.at[pl.ds(idx_smem[slot, 2 * r + k], 1)],
                                  sem).start(priority=k)
        return c

    lax.fori_loop(0, td, body, 0, unroll=8)

    @pl.when(i == 0)
    def _():
        zrow[...] = jnp.zeros(zrow.shape, zrow.dtype)
        nrange = pad_ref.shape[0] // 2
        for e in range(nrange):
            start = pad_ref[e]
            n = pad_ref[nrange + e]

            def zbody(r, c):
                pltpu.make_async_copy(zrow, xb_hbm.at[pl.ds(start + r, 1)], sem).start()
                return c

            lax.fori_loop(0, n, zbody, 0)

            def wbody(r, c):
                pltpu.make_async_copy(zrow, xb_hbm.at[pl.ds(0, 1)], sem).wait()
                return c

            lax.fori_loop(0, n, wbody, 0)

    for k in range(2):
        pltpu.make_async_copy(h_ref, xb_hbm.at[pl.ds(0, td)], sem).wait()


def moe_dispatch(h, dest, pad_info, *, cap, td):
    t, d = h.shape
    return pl.pallas_call(
        functools.partial(_dispatch_kernel, td=td),
        grid_spec=pltpu.PrefetchScalarGridSpec(
            num_scalar_prefetch=1,
            grid=(t // td,),
            in_specs=[
                pl.BlockSpec(memory_space=pl.ANY),
                pl.BlockSpec((td, d), lambda i, pad: (i, 0)),
            ],
            out_specs=pl.BlockSpec(memory_space=pl.ANY),
            scratch_shapes=[
                pltpu.SMEM((2, 2 * td), I32),
                pltpu.VMEM((1, d), h.dtype),
                pltpu.SemaphoreType.DMA((2,)),
                pltpu.SemaphoreType.DMA,
            ],
        ),
        out_shape=jax.ShapeDtypeStruct((cap, d), h.dtype),
        compiler_params=_params("arbitrary"),
        name="moe_dispatch",
    )(pad_info, dest, h)


def _experts_kernel(be_ref, x_ref, wg_ref, wu_ref, wd_ref, o_ref, xb_ref):
    i = pl.program_id(0)
    j = pl.program_id(1)
    used = i < be_ref[be_ref.shape[0] - 1]

    @pl.when(j == 0)
    def _():
        o_ref[...] = jnp.zeros(o_ref.shape, o_ref.dtype)

    @pl.when(used & (j == 0))
    def _():
        xb_ref[...] = x_ref[...].astype(BF16)

    @pl.when(used)
    def _():
        _swiglu_accumulate(xb_ref, wg_ref[0], wu_ref[0], wd_ref[0], o_ref)


def moe_experts(block_e, xb, wg, wu, wd, *, bm, tf):
    cap, d = xb.shape
    f = wg.shape[2]
    nblk = cap // bm
    return pl.pallas_call(
        _experts_kernel,
        grid_spec=pltpu.PrefetchScalarGridSpec(
            num_scalar_prefetch=1,
            grid=(nblk, f // tf),
            in_specs=[
                pl.BlockSpec((bm, d), lambda i, j, be: (jnp.minimum(i, be[nblk] - 1), 0)),
                pl.BlockSpec((1, d, tf), lambda i, j, be: (be[i], 0, j)),
                pl.BlockSpec((1, d, tf), lambda i, j, be: (be[i], 0, j)),
                pl.BlockSpec((1, tf, d), lambda i, j, be: (be[i], j, 0)),
            ],
            out_specs=pl.BlockSpec((bm, d), lambda i, j, be: (i, 0)),
            scratch_shapes=[pltpu.VMEM((bm, d), BF16)],
        ),
        out_shape=jax.ShapeDtypeStruct((cap, d), F32),
        compiler_params=_params("parallel", "arbitrary"),
        name="moe_experts",
    )(block_e, xb, wg, wu, wd)


def _combine_kernel(dest_hbm, yb_hbm, x_ref, gate_ref, o_ref, idx_smem, y0, y1, isem, sem, *, tc):
    i = pl.program_id(0)
    n = pl.num_programs(0)
    slot = i % 2
    nxt = 1 - slot

    def gather(s):
        def body(r, c):
            pltpu.make_async_copy(yb_hbm.at[pl.ds(idx_smem[s, 2 * r], 1)], y0.at[s, pl.ds(r, 1)],
                                  sem.at[s]).start(priority=0)
            pltpu.make_async_copy(yb_hbm.at[pl.ds(idx_smem[s, 2 * r + 1], 1)], y1.at[s, pl.ds(r, 1)],
                                  sem.at[s]).start(priority=1)
            return c

        lax.fori_loop(0, tc, body, 0, unroll=8)

    @pl.when(i == 0)
    def _():
        first = _index_copy(dest_hbm, idx_smem, isem, 0, 0)
        first.start()
        first.wait()
        gather(0)

        @pl.when(n > 1)
        def _():
            _index_copy(dest_hbm, idx_smem, isem, 1, 1).start()

    @pl.when(i + 1 < n)
    def _():
        _index_copy(dest_hbm, idx_smem, isem, i + 1, nxt).wait()

        @pl.when(i + 2 < n)
        def _():
            _index_copy(dest_hbm, idx_smem, isem, i + 2, slot).start()

        gather(nxt)

    for y in (y0, y1):
        pltpu.make_async_copy(yb_hbm.at[pl.ds(0, tc)], y.at[slot], sem.at[slot]).wait()
    gate = gate_ref[...]
    o_ref[...] = x_ref[...] + (gate[:, 0:1] * y0[slot] + gate[:, 1:2] * y1[slot])


def moe_combine(dest, yb, x, gates, *, tc):
    t, d = x.shape
    return pl.pallas_call(
        functools.partial(_combine_kernel, tc=tc),
        grid=(t // tc,),
        in_specs=[
            pl.BlockSpec(memory_space=pl.ANY),
            pl.BlockSpec(memory_space=pl.ANY),
            pl.BlockSpec((tc, d), lambda i: (i, 0)),
            pl.BlockSpec((tc, LANES), lambda i: (i, 0)),
        ],
        out_specs=pl.BlockSpec((tc, d), lambda i: (i, 0)),
        out_shape=jax.ShapeDtypeStruct((t, d), F32),
        scratch_shapes=[
            pltpu.SMEM((2, 2 * tc), I32),
            pltpu.VMEM((2, tc, d), F32),
            pltpu.VMEM((2, tc, d), F32),
            pltpu.SemaphoreType.DMA((2,)),
            pltpu.SemaphoreType.DMA((2,)),
        ],
        compiler_params=_params("arbitrary"),
        name="moe_combine",
    )(dest, yb, x, gates)


def _tile(n, pref):
    t = min(n, pref)
    while n % t:
        t //= 2
    return t


def even_layer(x, p, *, batch, seq):
    t, d = x.shape
    tm = _tile(t, 1024)
    proj, dt_raw = norm_proj(x, p["norm_mix"], p["w_in_main"], p["w_in_dt"], tm=tm, tn=p["tn_in"])
    mix = conv_ssd_mixer(proj, dt_raw, p["conv_w"], p["xw"], p["bcw"], p["xb"], p["bcb"], p["dt_bias"],
                         p["a_log"], p["d_skip"], p["ssd_norm_g"], p["expand"],
                         batch=batch, seq=seq, blk=_tile(seq, 256))
    x, h = proj_res_norm(mix, p["w_out"], x, p["norm_ffn"], tm=_tile(t, 512))
    return ffn(h, x, p["w_gate"], p["w_up"], p["w_down"], tm=tm, tf=FF_TILE)


def moe_layer(x, h, meta, counts, p, *, bm):
    t, d = x.shape
    rank = meta[:, 2:4].astype(I32)
    top_e = meta[:, 4:6].astype(I32)
    counts = counts[0, :N_EXPERTS].astype(I32)
    padded = (counts + bm - 1) // bm * bm
    pends = jnp.cumsum(padded)
    pstarts = pends - padded
    dest = pstarts[top_e] + rank
    nblk = (2 * t) // bm + N_EXPERTS
    cap = nblk * bm
    block_e = jnp.minimum(jnp.searchsorted(pends, jnp.arange(nblk, dtype=I32) * bm, side="right"),
                          N_EXPERTS - 1).astype(I32)
    block_e = jnp.concatenate([block_e, (pends[-1:] // bm).astype(I32)])
    pad_info = jnp.concatenate([pstarts + counts, pends[-1:], padded - counts, cap - pends[-1:]]).astype(I32)
    td = _tile(t, 512)
    dest_t = dest.reshape(t // td, 2 * td)
    xb = moe_dispatch(h, dest_t, pad_info, cap=cap, td=td)
    yb = moe_experts(block_e, xb, p["w_gate"], p["w_up"], p["w_down"], bm=bm, tf=FF_TILE)
    return moe_combine(dest_t, yb, x, meta, tc=td)


def odd_layer(x, p, *, batch, seq, lambda_init, cos_t, sin_t):
    t, d = x.shape
    tm = _tile(seq, 1024)
    qkv = qkv_proj(x, p["norm_mix"], p["w_qkv"], p["head_gain"], cos_t, sin_t, p["pavg"], tm=tm, seq=seq)
    o = diff_attention(qkv, p["lam"], p["sub_norm"], batch=batch, seq=seq, heads=d // LANES,
                       tq=_tile(seq, 512), lambda_init=lambda_init)
    x, h, meta, counts = proj_res_norm_route(o, p["w_o"], x, p["norm_ffn"], p["w_router"], tm=_tile(t, 512))
    return moe_layer(x, h, meta, counts, p, bm=_tile(t, 512))


def _rope_tables(seq):
    inv = 1.0 / (ROPE_THETA ** (jnp.arange(0, HEAD, 2, dtype=F32) / HEAD))
    ang = jnp.arange(seq, dtype=F32)[:, None] * inv[None, :]
    cos, sin = jnp.cos(ang), jnp.sin(ang)
    cos_t = jnp.concatenate([cos, cos, cos, cos], axis=1)
    sin_t = jnp.concatenate([-sin, sin, -sin, sin], axis=1)
    return cos_t, sin_t


def kernel(x, even_norm_mix, even_w_in, even_conv_w, ssd_conv_w, ssd_conv_b, ssd_dt_bias, ssd_a_log, ssd_d, ssd_norm_g, even_w_out, even_norm_ffn, ffn_w_gate, ffn_w_up, ffn_w_down, odd_norm_mix, attn_w_qkv, attn_q_norm, attn_k_norm, attn_lambda_q1, attn_lambda_k1, attn_lambda_q2, attn_lambda_k2, attn_sub_norm, attn_w_o, odd_norm_ffn, moe_w_router, moe_w_gate, moe_w_up, moe_w_down):
    batch, seq, d = x.shape
    depth = even_norm_mix.shape[0] + odd_norm_mix.shape[0]
    n_main = 5 * d + 2 * LANES
    heads = jnp.arange(d) // HEAD
    expand = (jnp.arange(LANES)[:, None] == heads[None, :]).astype(BF16)
    pavg = jnp.where(heads[:, None] == heads[None, :], 1.0 / HEAD, 0.0).astype(BF16)
    cos_t, sin_t = _rope_tables(seq)
    scale = math.log2(math.e) / math.sqrt(HEAD)

    def pad_lanes(v):
        return jnp.pad(v.astype(F32), (0, LANES - v.shape[0]))[None, :]

    xf = x.reshape(batch * seq, d)
    for layer in range(depth):
        i = layer // 2
        if layer % 2 == 0:
            w_in = even_w_in[i]
            p = dict(
                norm_mix=even_norm_mix[i][None, :],
                w_in_main=w_in[:, :n_main].astype(BF16),
                w_in_dt=jnp.pad(w_in[:, n_main:], ((0, 0), (0, LANES - (w_in.shape[1] - n_main)))).astype(BF16),
                tn_in=n_main // 3,
                conv_w=even_conv_w[i],
                xw=ssd_conv_w[i][:, :d], bcw=ssd_conv_w[i][:, d:],
                xb=ssd_conv_b[i][None, :d], bcb=ssd_conv_b[i][None, d:],
                dt_bias=pad_lanes(ssd_dt_bias[i]), a_log=pad_lanes(ssd_a_log[i]),
                d_skip=jnp.repeat(ssd_d[i].astype(F32), HEAD)[None, :],
                ssd_norm_g=ssd_norm_g[i][None, :], expand=expand,
                w_out=even_w_out[i].astype(BF16), norm_ffn=even_norm_ffn[i][None, :],
                w_gate=ffn_w_gate[i].astype(BF16), w_up=ffn_w_up[i].astype(BF16),
                w_down=ffn_w_down[i].astype(BF16),
            )
            xf = even_layer(xf, p, batch=batch, seq=seq)
        else:
            lambda_init = 0.8 - 0.6 * math.exp(-0.3 * layer)
            reps = d // HEAD
            head_gain = jnp.stack([jnp.tile(attn_q_norm[i], reps) * scale, jnp.tile(attn_k_norm[i], reps)])[:, None, :]
            lam = jnp.stack([pad_lanes(v)[0] for v in
                             (attn_lambda_q1[i], attn_lambda_k1[i], attn_lambda_q2[i], attn_lambda_k2[i])])
            p = dict(
                norm_mix=odd_norm_mix[i][None, :], w_qkv=attn_w_qkv[i].astype(BF16),
                head_gain=head_gain.astype(F32), pavg=pavg, lam=lam,
                sub_norm=attn_sub_norm[i][None, :], w_o=attn_w_o[i].astype(BF16),
                norm_ffn=odd_norm_ffn[i][None, :],
                w_router=jnp.pad(moe_w_router[i], ((0, 0), (0, LANES - N_EXPERTS))),
                w_gate=moe_w_gate[i].astype(BF16), w_up=moe_w_up[i].astype(BF16),
                w_down=moe_w_down[i].astype(BF16),
            )
            xf = odd_layer(xf, p, batch=batch, seq=seq, lambda_init=lambda_init, cos_t=cos_t, sin_t=sin_t)
    return xf.reshape(batch, seq, d)
```

```python
import functools
import math

import jax
import jax.numpy as jnp
from jax import lax
from jax.experimental import pallas as pl
from jax.experimental.pallas import tpu as pltpu

F32 = jnp.float32
BF16 = jnp.bfloat16
I32 = jnp.int32

EPS = 1e-6
ROPE_THETA = 10000.0
CHUNK = 64
HEAD = 64
LANES = 128
SSD_HEADS = 16
SSD_GROUPS = 2
N_EXPERTS = 8
VMEM_LIMIT = 56 * 1024 * 1024
NEG = -1e30


def _params(*sem):
    return pltpu.CompilerParams(dimension_semantics=sem, vmem_limit_bytes=VMEM_LIMIT)


def _rms(x, g):
    ms = jnp.mean(x * x, axis=-1, keepdims=True)
    return x * lax.rsqrt(ms + EPS) * g


def _silu(x):
    return x * (1.0 / (1.0 + jnp.exp(-x)))


def _dot(a, b):
    return jnp.dot(a, b, preferred_element_type=F32)


def _dot_nt(a, b):
    return lax.dot_general(a, b, (((1,), (1,)), ((), ())), preferred_element_type=F32)


def _dot_tn(a, b):
    return lax.dot_general(a, b, (((0,), (0,)), ((), ())), preferred_element_type=F32)


def _norm_proj_kernel(x_ref, g_ref, w_ref, wdt_ref, o_ref, odt_ref, h_ref):
    @pl.when(pl.program_id(1) == 0)
    def _():
        h = _rms(x_ref[...], g_ref[...]).astype(BF16)
        h_ref[...] = h
        odt_ref[...] = _dot(h, wdt_ref[...])

    o_ref[...] = _dot(h_ref[...], w_ref[...]).astype(o_ref.dtype)


def norm_proj(x, g, w_main, w_dt, *, tm, tn):
    t, d = x.shape
    n = w_main.shape[1]
    return pl.pallas_call(
        _norm_proj_kernel,
        grid=(t // tm, n // tn),
        in_specs=[
            pl.BlockSpec((tm, d), lambda i, j: (i, 0)),
            pl.BlockSpec((1, d), lambda i, j: (0, 0)),
            pl.BlockSpec((d, tn), lambda i, j: (0, j)),
            pl.BlockSpec((d, LANES), lambda i, j: (0, 0)),
        ],
        out_specs=[
            pl.BlockSpec((tm, tn), lambda i, j: (i, j)),
            pl.BlockSpec((tm, LANES), lambda i, j: (i, 0)),
        ],
        out_shape=[
            jax.ShapeDtypeStruct((t, n), BF16),
            jax.ShapeDtypeStruct((t, LANES), F32),
        ],
        scratch_shapes=[pltpu.VMEM((tm, d), BF16)],
        compiler_params=_params("parallel", "arbitrary"),
        name="norm_proj",
    )(x, g, w_main, w_dt)


HALO = 8


def _split3(v):
    hi = v.astype(BF16)
    r = v - hi.astype(F32)
    mid = r.astype(BF16)
    lo = (r - mid.astype(F32)).astype(BF16)
    return hi, mid, lo


def _expand(v, e, terms):
    parts = _split3(v)[:terms]
    out = _dot(parts[0], e)
    for p in parts[1:]:
        out = out + _dot(p, e)
    return out


def _mixer_kernel(bg_ref, cg_ref, hv_ref, z_ref, xs_ref, bc_ref, dtr_ref,
                  cw_ref, xw_ref, bcw_ref, xb_ref, bcb_ref, dtb_ref, alog_ref,
                  dskip_ref, ng_ref, e_ref,
                  o_ref, ubuf, xbuf, bcbuf, state, ybuf):
    L = o_ref.shape[0]
    d = bg_ref.shape[1]
    nchunk = L // CHUNK

    @pl.when(pl.program_id(1) == 0)
    def _():
        for buf in (ubuf, xbuf, bcbuf):
            buf[:, 0:HALO, :] = jnp.zeros((buf.shape[0], HALO, LANES), F32)
        state[...] = jnp.zeros(state.shape, F32)

    def causal_conv(v, buf, w):
        taps = w.shape[0]
        cols = []
        for c in range(v.shape[1] // LANES):
            lanes = slice(c * LANES, (c + 1) * LANES)
            vc = v[:, lanes]
            buf[c, HALO:HALO + L, :] = vc
            acc = w[taps - 1:taps, lanes] * vc
            for k in range(taps - 1):
                off = HALO - (taps - 1) + k
                acc = acc + w[k:k + 1, lanes] * buf[c, off:off + L, :]
            buf[c, 0:HALO, :] = buf[c, L:L + HALO, :]
            cols.append(acc)
        return jnp.concatenate(cols, axis=1)

    u = cg_ref[...].astype(F32) * hv_ref[...].astype(F32)
    conv = causal_conv(u, ubuf, cw_ref[...])
    o_ref[:, 0:d] = (bg_ref[...].astype(F32) * conv).astype(o_ref.dtype)

    xc = _silu(causal_conv(xs_ref[...].astype(F32), xbuf, xw_ref[...]) + xb_ref[...])
    bcc = _silu(causal_conv(bc_ref[...].astype(F32), bcbuf, bcw_ref[...]) + bcb_ref[...])

    e = e_ref[...]
    dtv = dtr_ref[...] + dtb_ref[...]
    dt = jnp.maximum(dtv, 0.0) + jnp.log(1.0 + jnp.exp(-jnp.abs(dtv)))
    a = -jnp.exp(alog_ref[...])
    da = dt * a

    ri = lax.broadcasted_iota(I32, (L, L), 0)
    ci = lax.broadcasted_iota(I32, (L, L), 1)
    tri = jnp.where((ri >= ci) & ((ri // CHUNK) == (ci // CHUNK)), 1.0, 0.0).astype(BF16)
    d_hi, d_mid, d_lo = _split3(da)
    acs = _dot(tri, d_hi) + _dot(tri, d_mid) + _dot(tri, d_lo)

    last = jnp.concatenate(
        [jnp.broadcast_to(acs[c * CHUNK + CHUNK - 1:c * CHUNK + CHUNK, :], (CHUNK, LANES))
         for c in range(nchunk)], axis=0)
    col_all = _expand(acs, e, 3)
    e_all = jnp.exp(col_all)
    ds_all = jnp.exp(_expand(last, e, 3) - col_all)
    xdt = xc * _expand(dt, e, 3)
    xwt = (xdt * ds_all).astype(BF16)
    xdt_b = xdt.astype(BF16)

    lane = lax.broadcasted_iota(I32, (CHUNK, LANES), 1)
    row = lax.broadcasted_iota(I32, (CHUNK, LANES), 0)
    low_half = lane < HEAD
    tri2 = row >= (lane % HEAD)
    lane2 = lax.broadcasted_iota(I32, (2 * CHUNK, LANES), 1)
    row2 = lax.broadcasted_iota(I32, (2 * CHUNK, LANES), 0)
    bd_mask = (lane2 < HEAD) == (row2 < CHUNK)
    srow = lax.broadcasted_iota(I32, state.shape, 0)
    scol = lax.broadcasted_iota(I32, state.shape, 1)
    state_mask = (srow // HEAD) == (scol // (d // SSD_GROUPS))

    for c in range(nchunk):
        r0 = c * CHUNK
        bm = bcc[r0:r0 + CHUNK, 0:LANES]
        cm = bcc[r0:r0 + CHUNK, LANES:2 * LANES]
        bm_b = bm.astype(BF16)
        cm_b = cm.astype(BF16)
        acs_c = acs[r0:r0 + CHUNK, :]
        acs_t = jnp.concatenate([acs_c, acs_c], axis=0).T
        st = state[...]
        y_off = _dot(cm_b, st.astype(BF16)) * e_all[r0:r0 + CHUNK, :]
        bm2 = jnp.concatenate([bm_b, bm_b], axis=0)
        cb2 = []
        for g in range(SSD_GROUPS):
            cmm = jnp.where((lane // HEAD) == g, cm, 0.0).astype(BF16)
            cb2.append(_dot_nt(cmm, bm2))
        yd = []
        for q in range(SSD_HEADS // 2):
            colq = col_all[r0:r0 + CHUNK, q * LANES:(q + 1) * LANES]
            rowq = jnp.where(low_half[0:1], acs_t[2 * q:2 * q + 1, :], acs_t[2 * q + 1:2 * q + 2, :])
            dec = jnp.exp(jnp.where(tri2, colq - rowq, NEG))
            m2 = (cb2[q // (SSD_HEADS // 2 // SSD_GROUPS)] * dec).astype(BF16)
            xq = xdt_b[r0:r0 + CHUNK, q * LANES:(q + 1) * LANES]
            xq2 = jnp.concatenate([xq, xq], axis=0)
            xbd = jnp.where(bd_mask, xq2, jnp.zeros_like(xq2))
            yd.append(_dot(m2, xbd))
        y = jnp.concatenate(yd, axis=1) + y_off + dskip_ref[...] * xc[r0:r0 + CHUNK, :]
        ybuf[r0:r0 + CHUNK, :] = y
        s_new = _dot_tn(bm_b, xwt[r0:r0 + CHUNK, :])
        state[...] = st * e_all[r0 + CHUNK - 1:r0 + CHUNK, :] + jnp.where(state_mask, s_new, 0.0)

    y = ybuf[...] * _silu(z_ref[...].astype(F32))
    half = d // SSD_GROUPS
    ng = ng_ref[...]
    for g in range(SSD_GROUPS):
        yg = y[:, g * half:(g + 1) * half]
        o_ref[:, d + g * half:d + (g + 1) * half] = _rms(yg, ng[:, g * half:(g + 1) * half]).astype(o_ref.dtype)


def conv_ssd_mixer(proj, dt_raw, cw, xw, bcw, xb, bcb, dtb, alog, dskip, ng, expand, *, batch, seq, blk):
    t = proj.shape[0]
    d = 1024
    nb = seq // blk
    row = lambda b, s: b * nb + s
    full = lambda shape: pl.BlockSpec(shape, lambda b, s: (0, 0))
    return pl.pallas_call(
        _mixer_kernel,
        grid=(batch, nb),
        in_specs=[
            pl.BlockSpec((blk, d), lambda b, s: (row(b, s), 0)),
            pl.BlockSpec((blk, d), lambda b, s: (row(b, s), 1)),
            pl.BlockSpec((blk, d), lambda b, s: (row(b, s), 2)),
            pl.BlockSpec((blk, d), lambda b, s: (row(b, s), 3)),
            pl.BlockSpec((blk, d), lambda b, s: (row(b, s), 4)),
            pl.BlockSpec((blk, 2 * LANES), lambda b, s: (row(b, s), 5 * d // (2 * LANES))),
            pl.BlockSpec((blk, LANES), lambda b, s: (row(b, s), 0)),
            full(cw.shape), full(xw.shape), full(bcw.shape), full(xb.shape), full(bcb.shape),
            full(dtb.shape), full(alog.shape), full(dskip.shape), full(ng.shape), full(expand.shape),
        ],
        out_specs=pl.BlockSpec((blk, 2 * d), lambda b, s: (row(b, s), 0)),
        out_shape=jax.ShapeDtypeStruct((t, 2 * d), BF16),
        scratch_shapes=[
            pltpu.VMEM((d // LANES, blk + HALO, LANES), F32),
            pltpu.VMEM((d // LANES, blk + HALO, LANES), F32),
            pltpu.VMEM((2, blk + HALO, LANES), F32),
            pltpu.VMEM((SSD_GROUPS * HEAD, d), F32),
            pltpu.VMEM((blk, d), F32),
        ],
        compiler_params=_params("parallel", "arbitrary"),
        name="conv_ssd_mixer",
    )(proj, proj, proj, proj, proj, proj, dt_raw, cw, xw, bcw, xb, bcb, dtb, alog, dskip, ng, expand)


def _proj_res_norm_kernel(a_ref, w_ref, x_ref, g_ref, xo_ref, ho_ref):
    y = x_ref[...] + _dot(a_ref[...], w_ref[...])
    xo_ref[...] = y
    ho_ref[...] = _rms(y, g_ref[...]).astype(ho_ref.dtype)


def proj_res_norm(a, w, x, g, *, tm):
    t, k = a.shape
    d = w.shape[1]
    return pl.pallas_call(
        _proj_res_norm_kernel,
        grid=(t // tm,),
        in_specs=[
            pl.BlockSpec((tm, k), lambda i: (i, 0)),
            pl.BlockSpec((k, d), lambda i: (0, 0)),
            pl.BlockSpec((tm, d), lambda i: (i, 0)),
            pl.BlockSpec((1, d), lambda i: (0, 0)),
        ],
        out_specs=[
            pl.BlockSpec((tm, d), lambda i: (i, 0)),
            pl.BlockSpec((tm, d), lambda i: (i, 0)),
        ],
        out_shape=[
            jax.ShapeDtypeStruct((t, d), F32),
            jax.ShapeDtypeStruct((t, d), BF16),
        ],
        compiler_params=_params("parallel"),
        name="proj_res_norm",
    )(a, w, x, g)


SWIGLU_ROWS = 512


def _swiglu(x_ref, wg, wu, wd, o_ref, res_ref=None):
    rows = min(SWIGLU_ROWS, x_ref.shape[0])
    for r0 in range(0, x_ref.shape[0], rows):
        x = x_ref[r0:r0 + rows, :].astype(BF16)
        g = _dot(x, wg)
        u = _dot(x, wu)
        y = _dot((_silu(g) * u).astype(BF16), wd)
        o_ref[r0:r0 + rows, :] = y if res_ref is None else res_ref[r0:r0 + rows, :] + y


def _ffn_kernel(h_ref, x_ref, wg_ref, wu_ref, wd_ref, o_ref):
    _swiglu(h_ref, wg_ref[...], wu_ref[...], wd_ref[...], o_ref, x_ref)


def _resident(shape, index_map):
    return pl.BlockSpec(shape, index_map, pipeline_mode=pl.Buffered(1))


def ffn(h, x, wg, wu, wd, *, tm):
    t, d = x.shape
    f = wg.shape[1]
    return pl.pallas_call(
        _ffn_kernel,
        grid=(t // tm,),
        in_specs=[
            pl.BlockSpec((tm, d), lambda i: (i, 0)),
            pl.BlockSpec((tm, d), lambda i: (i, 0)),
            _resident((d, f), lambda i: (0, 0)),
            _resident((d, f), lambda i: (0, 0)),
            _resident((f, d), lambda i: (0, 0)),
        ],
        out_specs=pl.BlockSpec((tm, d), lambda i: (i, 0)),
        out_shape=jax.ShapeDtypeStruct((t, d), F32),
        compiler_params=_params("parallel"),
        name="ffn",
    )(h, x, wg, wu, wd)


def _qkv_kernel(x_ref, g_ref, w_ref, hg_ref, cos_ref, sin_ref, p_ref, o_ref, h_ref):
    j = pl.program_id(1)

    @pl.when(j == 0)
    def _():
        h_ref[...] = _rms(x_ref[...], g_ref[...]).astype(BF16)

    y = _dot(h_ref[...], w_ref[...])

    @pl.when(j < 2)
    def _():
        d = y.shape[1]
        ms = _dot((y * y).astype(BF16), p_ref[...])
        yn = y * lax.rsqrt(ms + EPS) * hg_ref[0]
        lane = lax.broadcasted_iota(I32, y.shape, 1)
        partner = jnp.where((lane % HEAD) < HEAD // 2,
                            pltpu.roll(yn, d - HEAD // 2, 1), pltpu.roll(yn, HEAD // 2, 1))
        reps = d // LANES
        cos = jnp.concatenate([cos_ref[...]] * reps, axis=1)
        sin = jnp.concatenate([sin_ref[...]] * reps, axis=1)
        o_ref[...] = (yn * cos + partner * sin).astype(o_ref.dtype)

    @pl.when(j == 2)
    def _():
        o_ref[...] = y.astype(o_ref.dtype)


def qkv_proj(x, g, w, head_gain, cos_t, sin_t, pavg, *, tm, seq):
    t, d = x.shape
    ns = seq // tm
    return pl.pallas_call(
        _qkv_kernel,
        grid=(t // tm, 3),
        in_specs=[
            pl.BlockSpec((tm, d), lambda i, j: (i, 0)),
            pl.BlockSpec((1, d), lambda i, j: (0, 0)),
            pl.BlockSpec((d, d), lambda i, j: (0, j)),
            pl.BlockSpec((1, 1, d), lambda i, j: (jnp.minimum(j, 1), 0, 0)),
            pl.BlockSpec((tm, LANES), lambda i, j: (i % ns, 0)),
            pl.BlockSpec((tm, LANES), lambda i, j: (i % ns, 0)),
            pl.BlockSpec((d, d), lambda i, j: (0, 0)),
        ],
        out_specs=pl.BlockSpec((tm, d), lambda i, j: (i, j)),
        out_shape=jax.ShapeDtypeStruct((t, 3 * d), BF16),
        scratch_shapes=[pltpu.VMEM((tm, d), BF16)],
        compiler_params=_params("parallel", "arbitrary"),
        name="qkv_proj",
    )(x, g, w, head_gain, cos_t, sin_t, pavg)


def _attn_kernel(q_ref, k_ref, v_ref, lam_ref, sn_ref, o_ref, m_ref, l_ref, acc_ref, s0_ref, s1_ref, *,
                 tq, lambda_init):
    seq = q_ref.shape[0]
    nq = seq // tq
    lane = lax.broadcasted_iota(I32, (tq, LANES), 1)
    lam_p = lam_ref[...]
    lam = (jnp.exp(jnp.sum(lam_p[0:1] * lam_p[1:2], axis=-1, keepdims=True))
           - jnp.exp(jnp.sum(lam_p[2:3] * lam_p[3:4], axis=-1, keepdims=True)) + lambda_init)

    def stacked_q(qi):
        q = q_ref[qi * tq:(qi + 1) * tq, :]
        zero = jnp.zeros_like(q)
        return jnp.concatenate([jnp.where(lane < HEAD, q, zero), jnp.where(lane >= HEAD, q, zero)], axis=0)

    def scores(step, dst):
        qi, kb = step
        dst[...] = _dot_nt(k_ref[kb * tq:(kb + 1) * tq, :], stacked_q(qi))

    def update(step, src):
        qi, kb = step
        s = src[...]
        if kb == qi:
            ki = lax.broadcasted_iota(I32, s.shape, 0)
            qpos = lax.broadcasted_iota(I32, s.shape, 1) % tq
            s = jnp.where((ki // CHUNK) <= (qpos // CHUNK), s, NEG)
        if kb == 0:
            m_old = jnp.full(m_ref.shape, NEG, F32)
            l_old = jnp.zeros(l_ref.shape, F32)
            acc_old = jnp.zeros(acc_ref.shape, F32)
        else:
            m_old, l_old, acc_old = m_ref[...], l_ref[...], acc_ref[...]
        m_new = jnp.maximum(m_old, jnp.max(s, axis=0, keepdims=True))
        alpha = jnp.exp2(m_old - m_new)
        p = jnp.exp2(s - m_new)
        l_new = alpha * l_old + jnp.sum(p, axis=0, keepdims=True)
        acc_new = alpha * acc_old + _dot_tn(v_ref[kb * tq:(kb + 1) * tq, :], p.astype(BF16))
        if kb < qi:
            m_ref[...], l_ref[...], acc_ref[...] = m_new, l_new, acc_new
        else:
            o = acc_new / l_new
            o = (o[:, 0:tq] - lam * o[:, tq:2 * tq]).T
            o_ref[qi * tq:(qi + 1) * tq, :] = (_rms(o, sn_ref[...]) * (1.0 - lambda_init)).astype(o_ref.dtype)

    steps = [(qi, kb) for qi in range(nq) for kb in range(qi + 1)]
    bufs = (s0_ref, s1_ref)
    scores(steps[0], bufs[0])
    for n, step in enumerate(steps):
        if n + 1 < len(steps):
            scores(steps[n + 1], bufs[(n + 1) % 2])
        update(step, bufs[n % 2])


def diff_attention(qkv, lam_p, sub_norm, *, batch, seq, heads, tq, lambda_init):
    t = qkv.shape[0]
    return pl.pallas_call(
        functools.partial(_attn_kernel, tq=tq, lambda_init=lambda_init),
        grid=(batch, heads),
        in_specs=[
            pl.BlockSpec((seq, LANES), lambda b, h: (b, h)),
            pl.BlockSpec((seq, LANES), lambda b, h: (b, heads + h)),
            pl.BlockSpec((seq, LANES), lambda b, h: (b, 2 * heads + h)),
            pl.BlockSpec(lam_p.shape, lambda b, h: (0, 0)),
            pl.BlockSpec(sub_norm.shape, lambda b, h: (0, 0)),
        ],
        out_specs=pl.BlockSpec((seq, LANES), lambda b, h: (b, h)),
        out_shape=jax.ShapeDtypeStruct((t, heads * LANES), BF16),
        scratch_shapes=[
            pltpu.VMEM((1, 2 * tq), F32),
            pltpu.VMEM((1, 2 * tq), F32),
            pltpu.VMEM((LANES, 2 * tq), F32),
            pltpu.VMEM((tq, 2 * tq), F32),
            pltpu.VMEM((tq, 2 * tq), F32),
        ],
        compiler_params=_params("parallel", "parallel"),
        name="diff_attention",
    )(qkv, qkv, qkv, lam_p, sub_norm)


def _route(h, w, meta_ref, cnt_ref, run_ref):
    tm = h.shape[0]
    h_hi = h.astype(BF16)
    h_lo = (h - h_hi.astype(F32)).astype(BF16)
    w_hi = w.astype(BF16)
    w_lo = (w - w_hi.astype(F32)).astype(BF16)
    logits = _dot(h_hi, w_hi) + (_dot(h_hi, w_lo) + _dot(h_lo, w_hi))
    lane = lax.broadcasted_iota(I32, logits.shape, 1)
    lane_f = lane.astype(F32)
    logits = jnp.where(lane < N_EXPERTS, logits, NEG)
    m1 = jnp.max(logits, axis=-1, keepdims=True)
    i1 = jnp.min(jnp.where(logits == m1, lane_f, float(LANES)), axis=-1, keepdims=True)
    rest = jnp.where(lane_f == i1, NEG, logits)
    m2 = jnp.max(rest, axis=-1, keepdims=True)
    i2 = jnp.min(jnp.where(rest == m2, lane_f, float(LANES)), axis=-1, keepdims=True)
    e2 = jnp.exp(m2 - m1)
    g1 = 1.0 / (1.0 + e2)
    g2 = e2 * g1

    oh1 = lane_f == i1
    oh2 = lane_f == i2
    both = jnp.where(oh1, 1.0, 0.0) + jnp.where(oh2, 1.0, 0.0)
    ri = lax.broadcasted_iota(I32, (tm, tm), 0)
    ci = lax.broadcasted_iota(I32, (tm, tm), 1)
    strict = jnp.where(ri > ci, 1.0, 0.0).astype(BF16)
    before = run_ref[...] + _dot(strict, both.astype(BF16))
    r1 = jnp.sum(jnp.where(oh1, before, 0.0), axis=-1, keepdims=True)
    r2 = jnp.sum(jnp.where(oh2, before, 0.0), axis=-1, keepdims=True)
    run_ref[...] += jnp.sum(both, axis=0, keepdims=True)
    cnt_ref[...] = run_ref[...]

    meta = jnp.where(lane == 0, g1, jnp.where(lane == 1, g2, 0.0))
    meta = jnp.where(lane == 2, r1, jnp.where(lane == 3, r2, meta))
    meta_ref[...] = jnp.where(lane == 4, i1, jnp.where(lane == 5, i2, meta))


def _proj_res_norm_route_kernel(a_ref, w_ref, x_ref, g_ref, wr_ref, xo_ref, ho_ref, meta_ref, cnt_ref, run_ref):
    @pl.when(pl.program_id(0) == 0)
    def _():
        run_ref[...] = jnp.zeros(run_ref.shape, F32)

    y = x_ref[...] + _dot(a_ref[...], w_ref[...])
    xo_ref[...] = y
    h = _rms(y, g_ref[...])
    ho_ref[...] = h
    _route(h, wr_ref[...], meta_ref, cnt_ref, run_ref)


def proj_res_norm_route(a, w, x, g, w_router, *, tm):
    t, k = a.shape
    d = w.shape[1]
    return pl.pallas_call(
        _proj_res_norm_route_kernel,
        grid=(t // tm,),
        in_specs=[
            pl.BlockSpec((tm, k), lambda i: (i, 0)),
            pl.BlockSpec((k, d), lambda i: (0, 0)),
            pl.BlockSpec((tm, d), lambda i: (i, 0)),
            pl.BlockSpec((1, d), lambda i: (0, 0)),
            pl.BlockSpec((d, LANES), lambda i: (0, 0)),
        ],
        out_specs=[
            pl.BlockSpec((tm, d), lambda i: (i, 0)),
            pl.BlockSpec((tm, d), lambda i: (i, 0)),
            pl.BlockSpec((tm, LANES), lambda i: (i, 0)),
            pl.BlockSpec((1, LANES), lambda i: (0, 0)),
        ],
        out_shape=[
            jax.ShapeDtypeStruct((t, d), F32),
            jax.ShapeDtypeStruct((t, d), F32),
            jax.ShapeDtypeStruct((t, LANES), F32),
            jax.ShapeDtypeStruct((1, LANES), F32),
        ],
        scratch_shapes=[pltpu.VMEM((1, LANES), F32)],
        compiler_params=_params("arbitrary"),
        name="proj_res_norm_route",
    )(a, w, x, g, w_router)


def _index_copy(dest_hbm, idx_smem, isem, tile, slot):
    return pltpu.make_async_copy(dest_hbm.at[tile], idx_smem.at[slot], isem.at[slot])


def _dispatch_kernel(pad_ref, dest_hbm, h_ref, xb_hbm, idx_smem, zrow, isem, sem, *, td):
    i = pl.program_id(0)
    slot = i % 2

    @pl.when(i == 0)
    def _():
        _index_copy(dest_hbm, idx_smem, isem, 0, 0).start()

    @pl.when(i + 1 < pl.num_programs(0))
    def _():
        _index_copy(dest_hbm, idx_smem, isem, i + 1, 1 - slot).start()

    _index_copy(dest_hbm, idx_smem, isem, i, slot).wait()

    def body(r, c):
        for k in range(2):
            pltpu.make_async_copy(h_ref.at[pl.ds(r, 1)], xb_hbm.at[pl.ds(idx_smem[slot, 2 * r + k], 1)],
                                  sem).start()
        return c

    lax.fori_loop(0, td, body, 0, unroll=8)

    @pl.when(i == 0)
    def _():
        zrow[...] = jnp.zeros(zrow.shape, zrow.dtype)
        nrange = pad_ref.shape[0] // 2
        for e in range(nrange):
            start = pad_ref[e]
            n = pad_ref[nrange + e]

            def zbody(r, c):
                pltpu.make_async_copy(zrow, xb_hbm.at[pl.ds(start + r, 1)], sem).start()
                return c

            lax.fori_loop(0, n, zbody, 0)

            def wbody(r, c):
                pltpu.make_async_copy(zrow, xb_hbm.at[pl.ds(0, 1)], sem).wait()
                return c

            lax.fori_loop(0, n, wbody, 0)

    for k in range(2):
        pltpu.make_async_copy(h_ref, xb_hbm.at[pl.ds(0, td)], sem).wait()


def moe_dispatch(h, dest, pad_info, *, cap, td):
    t, d = h.shape
    return pl.pallas_call(
        functools.partial(_dispatch_kernel, td=td),
        grid_spec=pltpu.PrefetchScalarGridSpec(
            num_scalar_prefetch=1,
            grid=(t // td,),
            in_specs=[
                pl.BlockSpec(memory_space=pl.ANY),
                pl.BlockSpec((td, d), lambda i, pad: (i, 0)),
            ],
            out_specs=pl.BlockSpec(memory_space=pl.ANY),
            scratch_shapes=[
                pltpu.SMEM((2, 2 * td), I32),
                pltpu.VMEM((1, d), h.dtype),
                pltpu.SemaphoreType.DMA((2,)),
                pltpu.SemaphoreType.DMA,
            ],
        ),
        out_shape=jax.ShapeDtypeStruct((cap, d), h.dtype),
        compiler_params=_params("arbitrary"),
        name="moe_dispatch",
    )(pad_info, dest, h)


def _experts_kernel(be_ref, x_ref, wg_ref, wu_ref, wd_ref, o_ref):
    used = pl.program_id(0) < be_ref[be_ref.shape[0] - 1]

    @pl.when(used)
    def _():
        _swiglu(x_ref, wg_ref[0], wu_ref[0], wd_ref[0], o_ref)

    @pl.when(jnp.logical_not(used))
    def _():
        o_ref[...] = jnp.zeros(o_ref.shape, o_ref.dtype)


def moe_experts(block_e, xb, wg, wu, wd, *, bm):
    cap, d = xb.shape
    f = wg.shape[2]
    nblk = cap // bm
    return pl.pallas_call(
        _experts_kernel,
        grid_spec=pltpu.PrefetchScalarGridSpec(
            num_scalar_prefetch=1,
            grid=(nblk,),
            in_specs=[
                pl.BlockSpec((bm, d), lambda i, be: (jnp.minimum(i, be[nblk] - 1), 0)),
                _resident((1, d, f), lambda i, be: (be[i], 0, 0)),
                _resident((1, d, f), lambda i, be: (be[i], 0, 0)),
                _resident((1, f, d), lambda i, be: (be[i], 0, 0)),
            ],
            out_specs=pl.BlockSpec((bm, d), lambda i, be: (i, 0)),
        ),
        out_shape=jax.ShapeDtypeStruct((cap, d), F32),
        compiler_params=_params("arbitrary"),
        name="moe_experts",
    )(block_e, xb, wg, wu, wd)


def _combine_kernel(dest_hbm, yb_hbm, x_ref, gate_ref, o_ref, idx_smem, y0, y1, isem, sem, *, tc):
    i = pl.program_id(0)
    n = pl.num_programs(0)
    slot = i % 2
    nxt = 1 - slot

    def gather(s):
        def body(r, c):
            pltpu.make_async_copy(yb_hbm.at[pl.ds(idx_smem[s, 2 * r], 1)], y0.at[s, pl.ds(r, 1)],
                                  sem.at[s]).start()
            pltpu.make_async_copy(yb_hbm.at[pl.ds(idx_smem[s, 2 * r + 1], 1)], y1.at[s, pl.ds(r, 1)],
                                  sem.at[s]).start()
            return c

        lax.fori_loop(0, tc, body, 0, unroll=8)

    @pl.when(i == 0)
    def _():
        first = _index_copy(dest_hbm, idx_smem, isem, 0, 0)
        first.start()
        first.wait()
        gather(0)

        @pl.when(n > 1)
        def _():
            _index_copy(dest_hbm, idx_smem, isem, 1, 1).start()

    @pl.when(i + 1 < n)
    def _():
        _index_copy(dest_hbm, idx_smem, isem, i + 1, nxt).wait()

        @pl.when(i + 2 < n)
        def _():
            _index_copy(dest_hbm, idx_smem, isem, i + 2, slot).start()

        gather(nxt)

    for y in (y0, y1):
        pltpu.make_async_copy(yb_hbm.at[pl.ds(0, tc)], y.at[slot], sem.at[slot]).wait()
    gate = gate_ref[...]
    o_ref[...] = x_ref[...] + (gate[:, 0:1] * y0[slot] + gate[:, 1:2] * y1[slot])


def moe_combine(dest, yb, x, gates, *, tc):
    t, d = x.shape
    return pl.pallas_call(
        functools.partial(_combine_kernel, tc=tc),
        grid=(t // tc,),
        in_specs=[
            pl.BlockSpec(memory_space=pl.ANY),
            pl.BlockSpec(memory_space=pl.ANY),
            pl.BlockSpec((tc, d), lambda i: (i, 0)),
            pl.BlockSpec((tc, LANES), lambda i: (i, 0)),
        ],
        out_specs=pl.BlockSpec((tc, d), lambda i: (i, 0)),
        out_shape=jax.ShapeDtypeStruct((t, d), F32),
        scratch_shapes=[
            pltpu.SMEM((2, 2 * tc), I32),
            pltpu.VMEM((2, tc, d), F32),
            pltpu.VMEM((2, tc, d), F32),
            pltpu.SemaphoreType.DMA((2,)),
            pltpu.SemaphoreType.DMA((2,)),
        ],
        compiler_params=_params("arbitrary"),
        name="moe_combine",
    )(dest, yb, x, gates)


def _tile(n, pref):
    t = min(n, pref)
    while n % t:
        t //= 2
    return t


def even_layer(x, p, *, batch, seq):
    t, d = x.shape
    tm = _tile(t, 1024)
    proj, dt_raw = norm_proj(x, p["norm_mix"], p["w_in_main"], p["w_in_dt"], tm=tm, tn=p["tn_in"])
    mix = conv_ssd_mixer(proj, dt_raw, p["conv_w"], p["xw"], p["bcw"], p["xb"], p["bcb"], p["dt_bias"],
                         p["a_log"], p["d_skip"], p["ssd_norm_g"], p["expand"],
                         batch=batch, seq=seq, blk=_tile(seq, 256))
    x, h = proj_res_norm(mix, p["w_out"], x, p["norm_ffn"], tm=_tile(t, 512))
    return ffn(h, x, p["w_gate"], p["w_up"], p["w_down"], tm=tm)


def moe_layer(x, h, meta, counts, p, *, bm):
    t, d = x.shape
    rank = meta[:, 2:4].astype(I32)
    top_e = meta[:, 4:6].astype(I32)
    counts = counts[0, :N_EXPERTS].astype(I32)
    padded = (counts + bm - 1) // bm * bm
    pends = jnp.cumsum(padded)
    pstarts = pends - padded
    dest = pstarts[top_e] + rank
    nblk = (2 * t) // bm + N_EXPERTS
    cap = nblk * bm
    block_e = jnp.minimum(jnp.searchsorted(pends, jnp.arange(nblk, dtype=I32) * bm, side="right"),
                          N_EXPERTS - 1).astype(I32)
    block_e = jnp.concatenate([block_e, (pends[-1:] // bm).astype(I32)])
    pad_info = jnp.concatenate([pstarts + counts, pends[-1:], padded - counts, cap - pends[-1:]]).astype(I32)
    td = _tile(t, 512)
    dest_t = dest.reshape(t // td, 2 * td)
    xb = moe_dispatch(h, dest_t, pad_info, cap=cap, td=td)
    yb = moe_experts(block_e, xb, p["w_gate"], p["w_up"], p["w_down"], bm=bm)
    return moe_combine(dest_t, yb, x, meta, tc=td)


def odd_layer(x, p, *, batch, seq, lambda_init, cos_t, sin_t):
    t, d = x.shape
    tm = _tile(seq, 1024)
    qkv = qkv_proj(x, p["norm_mix"], p["w_qkv"], p["head_gain"], cos_t, sin_t, p["pavg"], tm=tm, seq=seq)
    o = diff_attention(qkv, p["lam"], p["sub_norm"], batch=batch, seq=seq, heads=d // LANES,
                       tq=_tile(seq, 512), lambda_init=lambda_init)
    x, h, meta, counts = proj_res_norm_route(o, p["w_o"], x, p["norm_ffn"], p["w_router"], tm=_tile(t, 512))
    return moe_layer(x, h, meta, counts, p, bm=_tile(t, 512))


def _rope_tables(seq):
    inv = 1.0 / (ROPE_THETA ** (jnp.arange(0, HEAD, 2, dtype=F32) / HEAD))
    ang = jnp.arange(seq, dtype=F32)[:, None] * inv[None, :]
    cos, sin = jnp.cos(ang), jnp.sin(ang)
    cos_t = jnp.concatenate([cos, cos, cos, cos], axis=1)
    sin_t = jnp.concatenate([-sin, sin, -sin, sin], axis=1)
    return cos_t, sin_t


def kernel(x, even_norm_mix, even_w_in, even_conv_w, ssd_conv_w, ssd_conv_b, ssd_dt_bias, ssd_a_log, ssd_d, ssd_norm_g, even_w_out, even_norm_ffn, ffn_w_gate, ffn_w_up, ffn_w_down, odd_norm_mix, attn_w_qkv, attn_q_norm, attn_k_norm, attn_lambda_q1, attn_lambda_k1, attn_lambda_q2, attn_lambda_k2, attn_sub_norm, attn_w_o, odd_norm_ffn, moe_w_router, moe_w_gate, moe_w_up, moe_w_down):
    batch, seq, d = x.shape
    depth = even_norm_mix.shape[0] + odd_norm_mix.shape[0]
    n_main = 5 * d + 2 * LANES
    heads = jnp.arange(d) // HEAD
    expand = (jnp.arange(LANES)[:, None] == heads[None, :]).astype(BF16)
    pavg = jnp.where(heads[:, None] == heads[None, :], 1.0 / HEAD, 0.0).astype(BF16)
    cos_t, sin_t = _rope_tables(seq)
    scale = math.log2(math.e) / math.sqrt(HEAD)

    def pad_lanes(v):
        return jnp.pad(v.astype(F32), (0, LANES - v.shape[0]))[None, :]

    xf = x.reshape(batch * seq, d)
    for layer in range(depth):
        i = layer // 2
        if layer % 2 == 0:
            w_in = even_w_in[i]
            p = dict(
                norm_mix=even_norm_mix[i][None, :],
                w_in_main=w_in[:, :n_main].astype(BF16),
                w_in_dt=jnp.pad(w_in[:, n_main:], ((0, 0), (0, LANES - (w_in.shape[1] - n_main)))).astype(BF16),
                tn_in=n_main // 3,
                conv_w=even_conv_w[i],
                xw=ssd_conv_w[i][:, :d], bcw=ssd_conv_w[i][:, d:],
                xb=ssd_conv_b[i][None, :d], bcb=ssd_conv_b[i][None, d:],
                dt_bias=pad_lanes(ssd_dt_bias[i]), a_log=pad_lanes(ssd_a_log[i]),
                d_skip=jnp.repeat(ssd_d[i].astype(F32), HEAD)[None, :],
                ssd_norm_g=ssd_norm_g[i][None, :], expand=expand,
                w_out=even_w_out[i].astype(BF16), norm_ffn=even_norm_ffn[i][None, :],
                w_gate=ffn_w_gate[i].astype(BF16), w_up=ffn_w_up[i].astype(BF16),
                w_down=ffn_w_down[i].astype(BF16),
            )
            xf = even_layer(xf, p, batch=batch, seq=seq)
        else:
            lambda_init = 0.8 - 0.6 * math.exp(-0.3 * layer)
            reps = d // HEAD
            head_gain = jnp.stack([jnp.tile(attn_q_norm[i], reps) * scale, jnp.tile(attn_k_norm[i], reps)])[:, None, :]
            lam = jnp.stack([pad_lanes(v)[0] for v in
                             (attn_lambda_q1[i], attn_lambda_k1[i], attn_lambda_q2[i], attn_lambda_k2[i])])
            p = dict(
                norm_mix=odd_norm_mix[i][None, :], w_qkv=attn_w_qkv[i].astype(BF16),
                head_gain=head_gain.astype(F32), pavg=pavg, lam=lam,
                sub_norm=attn_sub_norm[i][None, :], w_o=attn_w_o[i].astype(BF16),
                norm_ffn=odd_norm_ffn[i][None, :],
                w_router=jnp.pad(moe_w_router[i], ((0, 0), (0, LANES - N_EXPERTS))),
                w_gate=moe_w_gate[i].astype(BF16), w_up=moe_w_up[i].astype(BF16),
                w_down=moe_w_down[i].astype(BF16),
            )
            xf = odd_layer(xf, p, batch=batch, seq=seq, lambda_init=lambda_init, cos_t=cos_t, sin_t=sin_t)
    return xf.reshape(batch, seq, d)
```

```python
import functools
import math

import jax
import jax.numpy as jnp
from jax import lax
from jax.experimental import pallas as pl
from jax.experimental.pallas import tpu as pltpu

F32 = jnp.float32
BF16 = jnp.bfloat16
I32 = jnp.int32

EPS = 1e-6
ROPE_THETA = 10000.0
CHUNK = 64
HEAD = 64
LANES = 128
SSD_HEADS = 16
SSD_GROUPS = 2
N_EXPERTS = 8
VMEM_LIMIT = 56 * 1024 * 1024
NEG = -1e30


def _params(*sem):
    return pltpu.CompilerParams(dimension_semantics=sem, vmem_limit_bytes=VMEM_LIMIT)


def _rms(x, g):
    ms = jnp.mean(x * x, axis=-1, keepdims=True)
    return x * lax.rsqrt(ms + EPS) * g


def _silu(x):
    return x * (1.0 / (1.0 + jnp.exp(-x)))


def _dot(a, b):
    return jnp.dot(a, b, preferred_element_type=F32)


def _dot_nt(a, b):
    return lax.dot_general(a, b, (((1,), (1,)), ((), ())), preferred_element_type=F32)


def _dot_tn(a, b):
    return lax.dot_general(a, b, (((0,), (0,)), ((), ())), preferred_element_type=F32)


def _norm_proj_kernel(x_ref, g_ref, w_ref, wdt_ref, o_ref, odt_ref, h_ref):
    @pl.when(pl.program_id(1) == 0)
    def _():
        h = _rms(x_ref[...], g_ref[...]).astype(BF16)
        h_ref[...] = h
        odt_ref[...] = _dot(h, wdt_ref[...])

    o_ref[...] = _dot(h_ref[...], w_ref[...]).astype(o_ref.dtype)


def norm_proj(x, g, w_main, w_dt, *, tm, tn):
    t, d = x.shape
    n = w_main.shape[1]
    return pl.pallas_call(
        _norm_proj_kernel,
        grid=(t // tm, n // tn),
        in_specs=[
            pl.BlockSpec((tm, d), lambda i, j: (i, 0)),
            pl.BlockSpec((1, d), lambda i, j: (0, 0)),
            pl.BlockSpec((d, tn), lambda i, j: (0, j)),
            pl.BlockSpec((d, LANES), lambda i, j: (0, 0)),
        ],
        out_specs=[
            pl.BlockSpec((tm, tn), lambda i, j: (i, j)),
            pl.BlockSpec((tm, LANES), lambda i, j: (i, 0)),
        ],
        out_shape=[
            jax.ShapeDtypeStruct((t, n), BF16),
            jax.ShapeDtypeStruct((t, LANES), F32),
        ],
        scratch_shapes=[pltpu.VMEM((tm, d), BF16)],
        compiler_params=_params("parallel", "arbitrary"),
        name="norm_proj",
    )(x, g, w_main, w_dt)


HALO = 8


def _split3(v):
    hi = v.astype(BF16)
    r = v - hi.astype(F32)
    mid = r.astype(BF16)
    lo = (r - mid.astype(F32)).astype(BF16)
    return hi, mid, lo


def _expand(v, e, terms):
    parts = _split3(v)[:terms]
    out = _dot(parts[0], e)
    for p in parts[1:]:
        out = out + _dot(p, e)
    return out


def _mixer_kernel(bg_ref, cg_ref, hv_ref, z_ref, xs_ref, bc_ref, dtr_ref,
                  cw_ref, xw_ref, bcw_ref, xb_ref, bcb_ref, dtb_ref, alog_ref,
                  dskip_ref, ng_ref, e_ref,
                  o_ref, ubuf, xbuf, bcbuf, state, ybuf):
    L = o_ref.shape[0]
    d = bg_ref.shape[1]
    nchunk = L // CHUNK

    @pl.when(pl.program_id(1) == 0)
    def _():
        for buf in (ubuf, xbuf, bcbuf):
            buf[:, 0:HALO, :] = jnp.zeros((buf.shape[0], HALO, LANES), F32)
        state[...] = jnp.zeros(state.shape, F32)

    def causal_conv(v, buf, w):
        taps = w.shape[0]
        cols = []
        for c in range(v.shape[1] // LANES):
            lanes = slice(c * LANES, (c + 1) * LANES)
            vc = v[:, lanes]
            buf[c, HALO:HALO + L, :] = vc
            acc = w[taps - 1:taps, lanes] * vc
            for k in range(taps - 1):
                off = HALO - (taps - 1) + k
                acc = acc + w[k:k + 1, lanes] * buf[c, off:off + L, :]
            buf[c, 0:HALO, :] = buf[c, L:L + HALO, :]
            cols.append(acc)
        return jnp.concatenate(cols, axis=1)

    u = cg_ref[...].astype(F32) * hv_ref[...].astype(F32)
    conv = causal_conv(u, ubuf, cw_ref[...])
    o_ref[:, 0:d] = (bg_ref[...].astype(F32) * conv).astype(o_ref.dtype)

    xc = _silu(causal_conv(xs_ref[...].astype(F32), xbuf, xw_ref[...]) + xb_ref[...])
    bcc = _silu(causal_conv(bc_ref[...].astype(F32), bcbuf, bcw_ref[...]) + bcb_ref[...])

    e = e_ref[...]
    dtv = dtr_ref[...] + dtb_ref[...]
    dt = jnp.maximum(dtv, 0.0) + jnp.log(1.0 + jnp.exp(-jnp.abs(dtv)))
    a = -jnp.exp(alog_ref[...])
    da = dt * a

    ri = lax.broadcasted_iota(I32, (L, L), 0)
    ci = lax.broadcasted_iota(I32, (L, L), 1)
    tri = jnp.where((ri >= ci) & ((ri // CHUNK) == (ci // CHUNK)), 1.0, 0.0).astype(BF16)
    d_hi, d_mid, d_lo = _split3(da)
    acs = _dot(tri, d_hi) + _dot(tri, d_mid) + _dot(tri, d_lo)

    last = jnp.concatenate(
        [jnp.broadcast_to(acs[c * CHUNK + CHUNK - 1:c * CHUNK + CHUNK, :], (CHUNK, LANES))
         for c in range(nchunk)], axis=0)
    col_all = _expand(acs, e, 3)
    e_all = jnp.exp(col_all)
    ds_all = jnp.exp(_expand(last, e, 3) - col_all)
    xdt = xc * _expand(dt, e, 3)
    xwt = (xdt * ds_all).astype(BF16)
    xdt_b = xdt.astype(BF16)

    lane = lax.broadcasted_iota(I32, (CHUNK, LANES), 1)
    row = lax.broadcasted_iota(I32, (CHUNK, LANES), 0)
    low_half = lane < HEAD
    tri2 = row >= (lane % HEAD)
    lane2 = lax.broadcasted_iota(I32, (2 * CHUNK, LANES), 1)
    row2 = lax.broadcasted_iota(I32, (2 * CHUNK, LANES), 0)
    bd_mask = (lane2 < HEAD) == (row2 < CHUNK)
    srow = lax.broadcasted_iota(I32, state.shape, 0)
    scol = lax.broadcasted_iota(I32, state.shape, 1)
    state_mask = (srow // HEAD) == (scol // (d // SSD_GROUPS))

    for c in range(nchunk):
        r0 = c * CHUNK
        bm = bcc[r0:r0 + CHUNK, 0:LANES]
        cm = bcc[r0:r0 + CHUNK, LANES:2 * LANES]
        bm_b = bm.astype(BF16)
        cm_b = cm.astype(BF16)
        acs_c = acs[r0:r0 + CHUNK, :]
        acs_t = jnp.concatenate([acs_c, acs_c], axis=0).T
        st = state[...]
        y_off = _dot(cm_b, st.astype(BF16)) * e_all[r0:r0 + CHUNK, :]
        bm2 = jnp.concatenate([bm_b, bm_b], axis=0)
        cb2 = []
        for g in range(SSD_GROUPS):
            cmm = jnp.where((lane // HEAD) == g, cm, 0.0).astype(BF16)
            cb2.append(_dot_nt(cmm, bm2))
        yd = []
        for q in range(SSD_HEADS // 2):
            colq = col_all[r0:r0 + CHUNK, q * LANES:(q + 1) * LANES]
            rowq = jnp.where(low_half[0:1], acs_t[2 * q:2 * q + 1, :], acs_t[2 * q + 1:2 * q + 2, :])
            dec = jnp.exp(jnp.where(tri2, colq - rowq, NEG))
            m2 = (cb2[q // (SSD_HEADS // 2 // SSD_GROUPS)] * dec).astype(BF16)
            xq = xdt_b[r0:r0 + CHUNK, q * LANES:(q + 1) * LANES]
            xq2 = jnp.concatenate([xq, xq], axis=0)
            xbd = jnp.where(bd_mask, xq2, jnp.zeros_like(xq2))
            yd.append(_dot(m2, xbd))
        y = jnp.concatenate(yd, axis=1) + y_off + dskip_ref[...] * xc[r0:r0 + CHUNK, :]
        ybuf[r0:r0 + CHUNK, :] = y
        s_new = _dot_tn(bm_b, xwt[r0:r0 + CHUNK, :])
        state[...] = st * e_all[r0 + CHUNK - 1:r0 + CHUNK, :] + jnp.where(state_mask, s_new, 0.0)

    y = ybuf[...] * _silu(z_ref[...].astype(F32))
    half = d // SSD_GROUPS
    ng = ng_ref[...]
    for g in range(SSD_GROUPS):
        yg = y[:, g * half:(g + 1) * half]
        o_ref[:, d + g * half:d + (g + 1) * half] = _rms(yg, ng[:, g * half:(g + 1) * half]).astype(o_ref.dtype)


def conv_ssd_mixer(proj, dt_raw, cw, xw, bcw, xb, bcb, dtb, alog, dskip, ng, expand, *, batch, seq, blk):
    t = proj.shape[0]
    d = 1024
    nb = seq // blk
    row = lambda b, s: b * nb + s
    full = lambda shape: pl.BlockSpec(shape, lambda b, s: (0, 0))
    return pl.pallas_call(
        _mixer_kernel,
        grid=(batch, nb),
        in_specs=[
            pl.BlockSpec((blk, d), lambda b, s: (row(b, s), 0)),
            pl.BlockSpec((blk, d), lambda b, s: (row(b, s), 1)),
            pl.BlockSpec((blk, d), lambda b, s: (row(b, s), 2)),
            pl.BlockSpec((blk, d), lambda b, s: (row(b, s), 3)),
            pl.BlockSpec((blk, d), lambda b, s: (row(b, s), 4)),
            pl.BlockSpec((blk, 2 * LANES), lambda b, s: (row(b, s), 5 * d // (2 * LANES))),
            pl.BlockSpec((blk, LANES), lambda b, s: (row(b, s), 0)),
            full(cw.shape), full(xw.shape), full(bcw.shape), full(xb.shape), full(bcb.shape),
            full(dtb.shape), full(alog.shape), full(dskip.shape), full(ng.shape), full(expand.shape),
        ],
        out_specs=pl.BlockSpec((blk, 2 * d), lambda b, s: (row(b, s), 0)),
        out_shape=jax.ShapeDtypeStruct((t, 2 * d), BF16),
        scratch_shapes=[
            pltpu.VMEM((d // LANES, blk + HALO, LANES), F32),
            pltpu.VMEM((d // LANES, blk + HALO, LANES), F32),
            pltpu.VMEM((2, blk + HALO, LANES), F32),
            pltpu.VMEM((SSD_GROUPS * HEAD, d), F32),
            pltpu.VMEM((blk, d), F32),
        ],
        compiler_params=_params("parallel", "arbitrary"),
        name="conv_ssd_mixer",
    )(proj, proj, proj, proj, proj, proj, dt_raw, cw, xw, bcw, xb, bcb, dtb, alog, dskip, ng, expand)


def _proj_res_norm_kernel(a_ref, w_ref, x_ref, g_ref, xo_ref, ho_ref):
    y = x_ref[...] + _dot(a_ref[...], w_ref[...])
    xo_ref[...] = y
    ho_ref[...] = _rms(y, g_ref[...]).astype(ho_ref.dtype)


def proj_res_norm(a, w, x, g, *, tm):
    t, k = a.shape
    d = w.shape[1]
    return pl.pallas_call(
        _proj_res_norm_kernel,
        grid=(t // tm,),
        in_specs=[
            pl.BlockSpec((tm, k), lambda i: (i, 0)),
            pl.BlockSpec((k, d), lambda i: (0, 0)),
            pl.BlockSpec((tm, d), lambda i: (i, 0)),
            pl.BlockSpec((1, d), lambda i: (0, 0)),
        ],
        out_specs=[
            pl.BlockSpec((tm, d), lambda i: (i, 0)),
            pl.BlockSpec((tm, d), lambda i: (i, 0)),
        ],
        out_shape=[
            jax.ShapeDtypeStruct((t, d), F32),
            jax.ShapeDtypeStruct((t, d), BF16),
        ],
        compiler_params=_params("parallel"),
        name="proj_res_norm",
    )(a, w, x, g)


SWIGLU_ROWS = 512


def _swiglu(x_ref, wg, wu, wd, o_ref, res_ref=None):
    rows = min(SWIGLU_ROWS, x_ref.shape[0])
    for r0 in range(0, x_ref.shape[0], rows):
        x = x_ref[r0:r0 + rows, :].astype(BF16)
        g = _dot(x, wg)
        u = _dot(x, wu)
        y = _dot((_silu(g) * u).astype(BF16), wd)
        o_ref[r0:r0 + rows, :] = y if res_ref is None else res_ref[r0:r0 + rows, :] + y


def _ffn_kernel(h_ref, x_ref, wg_ref, wu_ref, wd_ref, o_ref):
    _swiglu(h_ref, wg_ref[...], wu_ref[...], wd_ref[...], o_ref, x_ref)


def _resident(shape, index_map):
    return pl.BlockSpec(shape, index_map, pipeline_mode=pl.Buffered(1))


def ffn(h, x, wg, wu, wd, *, tm):
    t, d = x.shape
    f = wg.shape[1]
    return pl.pallas_call(
        _ffn_kernel,
        grid=(t // tm,),
        in_specs=[
            pl.BlockSpec((tm, d), lambda i: (i, 0)),
            pl.BlockSpec((tm, d), lambda i: (i, 0)),
            _resident((d, f), lambda i: (0, 0)),
            _resident((d, f), lambda i: (0, 0)),
            _resident((f, d), lambda i: (0, 0)),
        ],
        out_specs=pl.BlockSpec((tm, d), lambda i: (i, 0)),
        out_shape=jax.ShapeDtypeStruct((t, d), F32),
        compiler_params=_params("parallel"),
        name="ffn",
    )(h, x, wg, wu, wd)


def _qkv_kernel(x_ref, g_ref, w_ref, hg_ref, cos_ref, sin_ref, p_ref, o_ref, h_ref):
    j = pl.program_id(1)

    @pl.when(j == 0)
    def _():
        h_ref[...] = _rms(x_ref[...], g_ref[...]).astype(BF16)

    y = _dot(h_ref[...], w_ref[...])

    @pl.when(j < 2)
    def _():
        d = y.shape[1]
        ms = _dot((y * y).astype(BF16), p_ref[...])
        yn = y * lax.rsqrt(ms + EPS) * hg_ref[0]
        lane = lax.broadcasted_iota(I32, y.shape, 1)
        partner = jnp.where((lane % HEAD) < HEAD // 2,
                            pltpu.roll(yn, d - HEAD // 2, 1), pltpu.roll(yn, HEAD // 2, 1))
        reps = d // LANES
        cos = jnp.concatenate([cos_ref[...]] * reps, axis=1)
        sin = jnp.concatenate([sin_ref[...]] * reps, axis=1)
        o_ref[...] = (yn * cos + partner * sin).astype(o_ref.dtype)

    @pl.when(j == 2)
    def _():
        o_ref[...] = y.astype(o_ref.dtype)


def qkv_proj(x, g, w, head_gain, cos_t, sin_t, pavg, *, tm, seq):
    t, d = x.shape
    ns = seq // tm
    return pl.pallas_call(
        _qkv_kernel,
        grid=(t // tm, 3),
        in_specs=[
            pl.BlockSpec((tm, d), lambda i, j: (i, 0)),
            pl.BlockSpec((1, d), lambda i, j: (0, 0)),
            pl.BlockSpec((d, d), lambda i, j: (0, j)),
            pl.BlockSpec((1, 1, d), lambda i, j: (jnp.minimum(j, 1), 0, 0)),
            pl.BlockSpec((tm, LANES), lambda i, j: (i % ns, 0)),
            pl.BlockSpec((tm, LANES), lambda i, j: (i % ns, 0)),
            pl.BlockSpec((d, d), lambda i, j: (0, 0)),
        ],
        out_specs=pl.BlockSpec((tm, d), lambda i, j: (i, j)),
        out_shape=jax.ShapeDtypeStruct((t, 3 * d), BF16),
        scratch_shapes=[pltpu.VMEM((tm, d), BF16)],
        compiler_params=_params("parallel", "arbitrary"),
        name="qkv_proj",
    )(x, g, w, head_gain, cos_t, sin_t, pavg)


def _attn_kernel(q_ref, k_ref, v_ref, lam_ref, sn_ref, o_ref, m_ref, l_ref, acc_ref, s0_ref, s1_ref, *,
                 tq, lambda_init):
    seq = q_ref.shape[0]
    nq = seq // tq
    lane = lax.broadcasted_iota(I32, (tq, LANES), 1)
    lam_p = lam_ref[...]
    lam = (jnp.exp(jnp.sum(lam_p[0:1] * lam_p[1:2], axis=-1, keepdims=True))
           - jnp.exp(jnp.sum(lam_p[2:3] * lam_p[3:4], axis=-1, keepdims=True)) + lambda_init)

    def stacked_q(qi):
        q = q_ref[qi * tq:(qi + 1) * tq, :]
        zero = jnp.zeros_like(q)
        return jnp.concatenate([jnp.where(lane < HEAD, q, zero), jnp.where(lane >= HEAD, q, zero)], axis=0)

    def scores(step, dst):
        qi, kb = step
        dst[...] = _dot_nt(k_ref[kb * tq:(kb + 1) * tq, :], stacked_q(qi))

    def update(step, src):
        qi, kb = step
        s = src[...]
        if kb == qi:
            ki = lax.broadcasted_iota(I32, s.shape, 0)
            qpos = lax.broadcasted_iota(I32, s.shape, 1) % tq
            s = jnp.where((ki // CHUNK) <= (qpos // CHUNK), s, NEG)
        if kb == 0:
            m_old = jnp.full(m_ref.shape, NEG, F32)
            l_old = jnp.zeros(l_ref.shape, F32)
            acc_old = jnp.zeros(acc_ref.shape, F32)
        else:
            m_old, l_old, acc_old = m_ref[...], l_ref[...], acc_ref[...]
        m_new = jnp.maximum(m_old, jnp.max(s, axis=0, keepdims=True))
        alpha = jnp.exp2(m_old - m_new)
        p = jnp.exp2(s - m_new)
        l_new = alpha * l_old + jnp.sum(p, axis=0, keepdims=True)
        acc_new = alpha * acc_old + _dot_tn(v_ref[kb * tq:(kb + 1) * tq, :], p.astype(BF16))
        if kb < qi:
            m_ref[...], l_ref[...], acc_ref[...] = m_new, l_new, acc_new
        else:
            o = acc_new / l_new
            o = (o[:, 0:tq] - lam * o[:, tq:2 * tq]).T
            o_ref[qi * tq:(qi + 1) * tq, :] = (_rms(o, sn_ref[...]) * (1.0 - lambda_init)).astype(o_ref.dtype)

    steps = [(qi, kb) for qi in range(nq) for kb in range(qi + 1)]
    bufs = (s0_ref, s1_ref)
    scores(steps[0], bufs[0])
    for n, step in enumerate(steps):
        if n + 1 < len(steps):
            scores(steps[n + 1], bufs[(n + 1) % 2])
        update(step, bufs[n % 2])


def diff_attention(qkv, lam_p, sub_norm, *, batch, seq, heads, tq, lambda_init):
    t = qkv.shape[0]
    return pl.pallas_call(
        functools.partial(_attn_kernel, tq=tq, lambda_init=lambda_init),
        grid=(batch, heads),
        in_specs=[
            pl.BlockSpec((seq, LANES), lambda b, h: (b, h)),
            pl.BlockSpec((seq, LANES), lambda b, h: (b, heads + h)),
            pl.BlockSpec((seq, LANES), lambda b, h: (b, 2 * heads + h)),
            pl.BlockSpec(lam_p.shape, lambda b, h: (0, 0)),
            pl.BlockSpec(sub_norm.shape, lambda b, h: (0, 0)),
        ],
        out_specs=pl.BlockSpec((seq, LANES), lambda b, h: (b, h)),
        out_shape=jax.ShapeDtypeStruct((t, heads * LANES), BF16),
        scratch_shapes=[
            pltpu.VMEM((1, 2 * tq), F32),
            pltpu.VMEM((1, 2 * tq), F32),
            pltpu.VMEM((LANES, 2 * tq), F32),
            pltpu.VMEM((tq, 2 * tq), F32),
            pltpu.VMEM((tq, 2 * tq), F32),
        ],
        compiler_params=_params("parallel", "parallel"),
        name="diff_attention",
    )(qkv, qkv, qkv, lam_p, sub_norm)


def _route(h, w, meta_ref, cnt_ref, run_ref):
    tm = h.shape[0]
    h_hi = h.astype(BF16)
    h_lo = (h - h_hi.astype(F32)).astype(BF16)
    w_hi = w.astype(BF16)
    w_lo = (w - w_hi.astype(F32)).astype(BF16)
    logits = _dot(h_hi, w_hi) + (_dot(h_hi, w_lo) + _dot(h_lo, w_hi))
    lane = lax.broadcasted_iota(I32, logits.shape, 1)
    lane_f = lane.astype(F32)
    logits = jnp.where(lane < N_EXPERTS, logits, NEG)
    m1 = jnp.max(logits, axis=-1, keepdims=True)
    i1 = jnp.min(jnp.where(logits == m1, lane_f, float(LANES)), axis=-1, keepdims=True)
    rest = jnp.where(lane_f == i1, NEG, logits)
    m2 = jnp.max(rest, axis=-1, keepdims=True)
    i2 = jnp.min(jnp.where(rest == m2, lane_f, float(LANES)), axis=-1, keepdims=True)
    e2 = jnp.exp(m2 - m1)
    g1 = 1.0 / (1.0 + e2)
    g2 = e2 * g1

    oh1 = lane_f == i1
    oh2 = lane_f == i2
    both = jnp.where(oh1, 1.0, 0.0) + jnp.where(oh2, 1.0, 0.0)
    ri = lax.broadcasted_iota(I32, (tm, tm), 0)
    ci = lax.broadcasted_iota(I32, (tm, tm), 1)
    strict = jnp.where(ri > ci, 1.0, 0.0).astype(BF16)
    before = run_ref[...] + _dot(strict, both.astype(BF16))
    r1 = jnp.sum(jnp.where(oh1, before, 0.0), axis=-1, keepdims=True)
    r2 = jnp.sum(jnp.where(oh2, before, 0.0), axis=-1, keepdims=True)
    run_ref[...] += jnp.sum(both, axis=0, keepdims=True)
    cnt_ref[...] = run_ref[...]

    meta = jnp.where(lane == 0, g1, jnp.where(lane == 1, g2, 0.0))
    meta = jnp.where(lane == 2, r1, jnp.where(lane == 3, r2, meta))
    meta_ref[...] = jnp.where(lane == 4, i1, jnp.where(lane == 5, i2, meta))


def _proj_res_norm_route_kernel(a_ref, w_ref, x_ref, g_ref, wr_ref, xo_ref, ho_ref, meta_ref, cnt_ref, run_ref):
    @pl.when(pl.program_id(0) == 0)
    def _():
        run_ref[...] = jnp.zeros(run_ref.shape, F32)

    y = x_ref[...] + _dot(a_ref[...], w_ref[...])
    xo_ref[...] = y
    h = _rms(y, g_ref[...])
    ho_ref[...] = h
    _route(h, wr_ref[...], meta_ref, cnt_ref, run_ref)


def proj_res_norm_route(a, w, x, g, w_router, *, tm):
    t, k = a.shape
    d = w.shape[1]
    return pl.pallas_call(
        _proj_res_norm_route_kernel,
        grid=(t // tm,),
        in_specs=[
            pl.BlockSpec((tm, k), lambda i: (i, 0)),
            pl.BlockSpec((k, d), lambda i: (0, 0)),
            pl.BlockSpec((tm, d), lambda i: (i, 0)),
            pl.BlockSpec((1, d), lambda i: (0, 0)),
            pl.BlockSpec((d, LANES), lambda i: (0, 0)),
        ],
        out_specs=[
            pl.BlockSpec((tm, d), lambda i: (i, 0)),
            pl.BlockSpec((tm, d), lambda i: (i, 0)),
            pl.BlockSpec((tm, LANES), lambda i: (i, 0)),
            pl.BlockSpec((1, LANES), lambda i: (0, 0)),
        ],
        out_shape=[
            jax.ShapeDtypeStruct((t, d), F32),
            jax.ShapeDtypeStruct((t, d), F32),
            jax.ShapeDtypeStruct((t, LANES), F32),
            jax.ShapeDtypeStruct((1, LANES), F32),
        ],
        scratch_shapes=[pltpu.VMEM((1, LANES), F32)],
        compiler_params=_params("arbitrary"),
        name="proj_res_norm_route",
    )(a, w, x, g, w_router)


def _index_copy(dest_hbm, idx, isem, tile, s):
    return pltpu.make_async_copy(dest_hbm.at[tile], idx[s], isem.at[s])


def _for_parity(slot, fn):
    for s in range(2):
        pl.when(slot == s)(functools.partial(fn, s))


def _dispatch_kernel(pad_ref, dest_hbm, h_ref, xb_hbm, idx_a, idx_b, zrow, isem, sem, *, td):
    i = pl.program_id(0)
    idx = (idx_a, idx_b)

    def step(s):
        @pl.when(i == 0)
        def _():
            _index_copy(dest_hbm, idx, isem, 0, s).start()

        @pl.when(i + 1 < pl.num_programs(0))
        def _():
            _index_copy(dest_hbm, idx, isem, i + 1, 1 - s).start()

        _index_copy(dest_hbm, idx, isem, i, s).wait()

        def body(r, c):
            for k in range(2):
                pltpu.make_async_copy(h_ref.at[pl.ds(r, 1)], xb_hbm.at[pl.ds(idx[s][2 * r + k], 1)], sem).start()
            return c

        lax.fori_loop(0, td, body, 0, unroll=8)

    _for_parity(i % 2, step)

    @pl.when(i == 0)
    def _():
        zrow[...] = jnp.zeros(zrow.shape, zrow.dtype)
        nrange = pad_ref.shape[0] // 2
        for e in range(nrange):
            start = pad_ref[e]
            n = pad_ref[nrange + e]

            def zbody(r, c):
                pltpu.make_async_copy(zrow, xb_hbm.at[pl.ds(start + r, 1)], sem).start()
                return c

            lax.fori_loop(0, n, zbody, 0)

            def wbody(r, c):
                pltpu.make_async_copy(zrow, xb_hbm.at[pl.ds(0, 1)], sem).wait()
                return c

            lax.fori_loop(0, n, wbody, 0)

    for k in range(2):
        pltpu.make_async_copy(h_ref, xb_hbm.at[pl.ds(0, td)], sem).wait()


def moe_dispatch(h, dest, pad_info, *, cap, td):
    t, d = h.shape
    return pl.pallas_call(
        functools.partial(_dispatch_kernel, td=td),
        grid_spec=pltpu.PrefetchScalarGridSpec(
            num_scalar_prefetch=1,
            grid=(t // td,),
            in_specs=[
                pl.BlockSpec(memory_space=pl.ANY),
                pl.BlockSpec((td, d), lambda i, pad: (i, 0)),
            ],
            out_specs=pl.BlockSpec(memory_space=pl.ANY),
            scratch_shapes=[
                pltpu.SMEM((2 * td,), I32),
                pltpu.SMEM((2 * td,), I32),
                pltpu.VMEM((1, d), h.dtype),
                pltpu.SemaphoreType.DMA((2,)),
                pltpu.SemaphoreType.DMA,
            ],
        ),
        out_shape=jax.ShapeDtypeStruct((cap, d), h.dtype),
        compiler_params=_params("arbitrary"),
        name="moe_dispatch",
    )(pad_info, dest, h)


def _experts_kernel(be_ref, x_ref, wg_ref, wu_ref, wd_ref, o_ref):
    used = pl.program_id(0) < be_ref[be_ref.shape[0] - 1]

    @pl.when(used)
    def _():
        _swiglu(x_ref, wg_ref[0], wu_ref[0], wd_ref[0], o_ref)

    @pl.when(jnp.logical_not(used))
    def _():
        o_ref[...] = jnp.zeros(o_ref.shape, o_ref.dtype)


def moe_experts(block_e, xb, wg, wu, wd, *, bm):
    cap, d = xb.shape
    f = wg.shape[2]
    nblk = cap // bm
    return pl.pallas_call(
        _experts_kernel,
        grid_spec=pltpu.PrefetchScalarGridSpec(
            num_scalar_prefetch=1,
            grid=(nblk,),
            in_specs=[
                pl.BlockSpec((bm, d), lambda i, be: (jnp.minimum(i, be[nblk] - 1), 0)),
                _resident((1, d, f), lambda i, be: (be[i], 0, 0)),
                _resident((1, d, f), lambda i, be: (be[i], 0, 0)),
                _resident((1, f, d), lambda i, be: (be[i], 0, 0)),
            ],
            out_specs=pl.BlockSpec((bm, d), lambda i, be: (i, 0)),
        ),
        out_shape=jax.ShapeDtypeStruct((cap, d), F32),
        compiler_params=_params("arbitrary"),
        name="moe_experts",
    )(block_e, xb, wg, wu, wd)


def _combine_kernel(dest_hbm, yb_hbm, x_ref, gate_ref, o_ref, idx_a, idx_b, y0, y1, isem, sem, *, tc):
    i = pl.program_id(0)
    n = pl.num_programs(0)
    idx = (idx_a, idx_b)

    def gather(s):
        def body(r, c):
            pltpu.make_async_copy(yb_hbm.at[pl.ds(idx[s][2 * r], 1)], y0.at[s, pl.ds(r, 1)], sem.at[s]).start()
            pltpu.make_async_copy(yb_hbm.at[pl.ds(idx[s][2 * r + 1], 1)], y1.at[s, pl.ds(r, 1)], sem.at[s]).start()
            return c

        lax.fori_loop(0, tc, body, 0, unroll=8)

    def step(s):
        @pl.when(i == 0)
        def _():
            first = _index_copy(dest_hbm, idx, isem, 0, s)
            first.start()
            first.wait()
            gather(s)

            @pl.when(n > 1)
            def _():
                _index_copy(dest_hbm, idx, isem, 1, 1 - s).start()

        @pl.when(i + 1 < n)
        def _():
            _index_copy(dest_hbm, idx, isem, i + 1, 1 - s).wait()

            @pl.when(i + 2 < n)
            def _():
                _index_copy(dest_hbm, idx, isem, i + 2, s).start()

            gather(1 - s)

        for y in (y0, y1):
            pltpu.make_async_copy(yb_hbm.at[pl.ds(0, tc)], y.at[s], sem.at[s]).wait()
        gate = gate_ref[...]
        o_ref[...] = x_ref[...] + (gate[:, 0:1] * y0[s] + gate[:, 1:2] * y1[s])

    _for_parity(i % 2, step)


def moe_combine(dest, yb, x, gates, *, tc):
    t, d = x.shape
    return pl.pallas_call(
        functools.partial(_combine_kernel, tc=tc),
        grid=(t // tc,),
        in_specs=[
            pl.BlockSpec(memory_space=pl.ANY),
            pl.BlockSpec(memory_space=pl.ANY),
            pl.BlockSpec((tc, d), lambda i: (i, 0)),
            pl.BlockSpec((tc, LANES), lambda i: (i, 0)),
        ],
        out_specs=pl.BlockSpec((tc, d), lambda i: (i, 0)),
        out_shape=jax.ShapeDtypeStruct((t, d), F32),
        scratch_shapes=[
            pltpu.SMEM((2 * tc,), I32),
            pltpu.SMEM((2 * tc,), I32),
            pltpu.VMEM((2, tc, d), F32),
            pltpu.VMEM((2, tc, d), F32),
            pltpu.SemaphoreType.DMA((2,)),
            pltpu.SemaphoreType.DMA((2,)),
        ],
        compiler_params=_params("arbitrary"),
        name="moe_combine",
    )(dest, yb, x, gates)


def _tile(n, pref):
    t = min(n, pref)
    while n % t:
        t //= 2
    return t


def even_layer(x, p, *, batch, seq):
    t, d = x.shape
    tm = _tile(t, 1024)
    proj, dt_raw = norm_proj(x, p["norm_mix"], p["w_in_main"], p["w_in_dt"], tm=tm, tn=p["tn_in"])
    mix = conv_ssd_mixer(proj, dt_raw, p["conv_w"], p["xw"], p["bcw"], p["xb"], p["bcb"], p["dt_bias"],
                         p["a_log"], p["d_skip"], p["ssd_norm_g"], p["expand"],
                         batch=batch, seq=seq, blk=_tile(seq, 256))
    x, h = proj_res_norm(mix, p["w_out"], x, p["norm_ffn"], tm=_tile(t, 512))
    return ffn(h, x, p["w_gate"], p["w_up"], p["w_down"], tm=tm)


def moe_layer(x, h, meta, counts, p, *, bm):
    t, d = x.shape
    rank = meta[:, 2:4].astype(I32)
    top_e = meta[:, 4:6].astype(I32)
    counts = counts[0, :N_EXPERTS].astype(I32)
    padded = (counts + bm - 1) // bm * bm
    pends = jnp.cumsum(padded)
    pstarts = pends - padded
    dest = pstarts[top_e] + rank
    nblk = (2 * t) // bm + N_EXPERTS
    cap = nblk * bm
    block_e = jnp.minimum(jnp.searchsorted(pends, jnp.arange(nblk, dtype=I32) * bm, side="right"),
                          N_EXPERTS - 1).astype(I32)
    block_e = jnp.concatenate([block_e, (pends[-1:] // bm).astype(I32)])
    pad_info = jnp.concatenate([pstarts + counts, pends[-1:], padded - counts, cap - pends[-1:]]).astype(I32)
    td = _tile(t, 512)
    dest_t = dest.reshape(t // td, 2 * td)
    xb = moe_dispatch(h, dest_t, pad_info, cap=cap, td=td)
    yb = moe_experts(block_e, xb, p["w_gate"], p["w_up"], p["w_down"], bm=bm)
    return moe_combine(dest_t, yb, x, meta, tc=td)


def odd_layer(x, p, *, batch, seq, lambda_init, cos_t, sin_t):
    t, d = x.shape
    tm = _tile(seq, 1024)
    qkv = qkv_proj(x, p["norm_mix"], p["w_qkv"], p["head_gain"], cos_t, sin_t, p["pavg"], tm=tm, seq=seq)
    o = diff_attention(qkv, p["lam"], p["sub_norm"], batch=batch, seq=seq, heads=d // LANES,
                       tq=_tile(seq, 512), lambda_init=lambda_init)
    x, h, meta, counts = proj_res_norm_route(o, p["w_o"], x, p["norm_ffn"], p["w_router"], tm=_tile(t, 512))
    return moe_layer(x, h, meta, counts, p, bm=_tile(t, 512))


def _rope_tables(seq):
    inv = 1.0 / (ROPE_THETA ** (jnp.arange(0, HEAD, 2, dtype=F32) / HEAD))
    ang = jnp.arange(seq, dtype=F32)[:, None] * inv[None, :]
    cos, sin = jnp.cos(ang), jnp.sin(ang)
    cos_t = jnp.concatenate([cos, cos, cos, cos], axis=1)
    sin_t = jnp.concatenate([-sin, sin, -sin, sin], axis=1)
    return cos_t, sin_t


def kernel(x, even_norm_mix, even_w_in, even_conv_w, ssd_conv_w, ssd_conv_b, ssd_dt_bias, ssd_a_log, ssd_d, ssd_norm_g, even_w_out, even_norm_ffn, ffn_w_gate, ffn_w_up, ffn_w_down, odd_norm_mix, attn_w_qkv, attn_q_norm, attn_k_norm, attn_lambda_q1, attn_lambda_k1, attn_lambda_q2, attn_lambda_k2, attn_sub_norm, attn_w_o, odd_norm_ffn, moe_w_router, moe_w_gate, moe_w_up, moe_w_down):
    batch, seq, d = x.shape
    depth = even_norm_mix.shape[0] + odd_norm_mix.shape[0]
    n_main = 5 * d + 2 * LANES
    heads = jnp.arange(d) // HEAD
    expand = (jnp.arange(LANES)[:, None] == heads[None, :]).astype(BF16)
    pavg = jnp.where(heads[:, None] == heads[None, :], 1.0 / HEAD, 0.0).astype(BF16)
    cos_t, sin_t = _rope_tables(seq)
    scale = math.log2(math.e) / math.sqrt(HEAD)

    def pad_lanes(v):
        return jnp.pad(v.astype(F32), (0, LANES - v.shape[0]))[None, :]

    xf = x.reshape(batch * seq, d)
    for layer in range(depth):
        i = layer // 2
        if layer % 2 == 0:
            w_in = even_w_in[i]
            p = dict(
                norm_mix=even_norm_mix[i][None, :],
                w_in_main=w_in[:, :n_main].astype(BF16),
                w_in_dt=jnp.pad(w_in[:, n_main:], ((0, 0), (0, LANES - (w_in.shape[1] - n_main)))).astype(BF16),
                tn_in=n_main // 3,
                conv_w=even_conv_w[i],
                xw=ssd_conv_w[i][:, :d], bcw=ssd_conv_w[i][:, d:],
                xb=ssd_conv_b[i][None, :d], bcb=ssd_conv_b[i][None, d:],
                dt_bias=pad_lanes(ssd_dt_bias[i]), a_log=pad_lanes(ssd_a_log[i]),
                d_skip=jnp.repeat(ssd_d[i].astype(F32), HEAD)[None, :],
                ssd_norm_g=ssd_norm_g[i][None, :], expand=expand,
                w_out=even_w_out[i].astype(BF16), norm_ffn=even_norm_ffn[i][None, :],
                w_gate=ffn_w_gate[i].astype(BF16), w_up=ffn_w_up[i].astype(BF16),
                w_down=ffn_w_down[i].astype(BF16),
            )
            xf = even_layer(xf, p, batch=batch, seq=seq)
        else:
            lambda_init = 0.8 - 0.6 * math.exp(-0.3 * layer)
            reps = d // HEAD
            head_gain = jnp.stack([jnp.tile(attn_q_norm[i], reps) * scale, jnp.tile(attn_k_norm[i], reps)])[:, None, :]
            lam = jnp.stack([pad_lanes(v)[0] for v in
                             (attn_lambda_q1[i], attn_lambda_k1[i], attn_lambda_q2[i], attn_lambda_k2[i])])
            p = dict(
                norm_mix=odd_norm_mix[i][None, :], w_qkv=attn_w_qkv[i].astype(BF16),
                head_gain=head_gain.astype(F32), pavg=pavg, lam=lam,
                sub_norm=attn_sub_norm[i][None, :], w_o=attn_w_o[i].astype(BF16),
                norm_ffn=odd_norm_ffn[i][None, :],
                w_router=jnp.pad(moe_w_router[i], ((0, 0), (0, LANES - N_EXPERTS))),
                w_gate=moe_w_gate[i].astype(BF16), w_up=moe_w_up[i].astype(BF16),
                w_down=moe_w_down[i].astype(BF16),
            )
            xf = odd_layer(xf, p, batch=batch, seq=seq, lambda_init=lambda_init, cos_t=cos_t, sin_t=sin_t)
    return xf.reshape(batch, seq, d)
```
